```python
import math
import jax
import jax.numpy as jnp
from jax import lax
import numpy as np

D_MODEL = 1024
BATCH = 2
SEQ = 8192
DEPTH = 4
DEC_BATCH = 8
DEC_SEQ = 4096
PAST_LEN = 128

N_BRANCH = 4
BRANCH_W = 512
EPS = 1e-6
CHUNK = 64
CONV_W = 4
CONV_PAD = (2, 1)
LRU_W = BRANCH_W
LRU_HEADS = 8
LRU_BLOCK = LRU_W // LRU_HEADS
LRU_C = 8.0
GLA_HEADS = 4
GLA_DK = 64
GLA_DV = BRANCH_W // GLA_HEADS
GLA_RANK = 16
GLA_TAU = 16.0
SSD_HEADS = 8
SSD_P = BRANCH_W // SSD_HEADS
SSD_GROUPS = 2
SSD_N = 64
SSD_CONV_CH = BRANCH_W + 2 * SSD_GROUPS * SSD_N
HY_W = BRANCH_W
HY_SHORT = 3
HY_SHORT_PAD = (1, 1)
HY_EMB = 33
HY_BANDS = (HY_EMB - 1) // 2
HY_FH = 64
HY_MIN_DECAY = -math.log(1e-2) / 1.5
HY_MAX_DECAY = -math.log(1e-2) / 0.3
MEM_LEN = 256
XA_HEADS = 4
XA_HD = D_MODEL // XA_HEADS
N_EXPERTS = 16
EC_CAPACITY = 2
EXPERT_FF = D_MODEL

IN_SIZES = (
    LRU_W, LRU_W,
    GLA_HEADS * GLA_DK, GLA_HEADS * GLA_DK,
    GLA_HEADS * GLA_DV, GLA_HEADS * GLA_DV,
    2 * GLA_RANK,
    BRANCH_W, SSD_CONV_CH, 2 * SSD_HEADS,
    3 * HY_W,
    N_BRANCH * D_MODEL,
)
IN_COLS = sum(IN_SIZES)

kernel_name = "hybrid_bidir_encoder_rglru_gla_ssd_hyena_ec"


def _in_split_points():
    return [int(v) for v in np.cumsum(IN_SIZES)[:-1]]


def rms_norm(x, g):
    xf = x.astype(jnp.float32)
    y = xf * lax.rsqrt(jnp.mean(xf * xf, axis=-1, keepdims=True) + EPS)
    return (y * g.astype(jnp.float32)).astype(x.dtype)


def dw_conv(x, w, b, pad):
    y = lax.conv_general_dilated(
        x, w[:, None, :].astype(x.dtype), window_strides=(1,), padding=[pad],
        dimension_numbers=("NWC", "WIO", "NWC"), feature_group_count=x.shape[-1])
    return y + b.astype(x.dtype)


def flip_seq(t):
    return jnp.flip(t, axis=1)


def _linear_combine(left, right):
    a1, b1 = left
    a2, b2 = right
    return a1 * a2, a2 * b1 + b2


def carry_states(local, decay):
    def step(state, inp):
        d, l = inp
        return d * state + l, state
    init = jnp.zeros_like(local[:, 0])
    _, states = lax.scan(step, init, (jnp.moveaxis(decay, 1, 0), jnp.moveaxis(local, 1, 0)))
    return jnp.moveaxis(states, 0, 1)


def rglru_scan(xc, gate_w, gate_b, lam):
    bn, s, _ = xc.shape
    xh = xc.reshape(bn, s, LRU_HEADS, LRU_BLOCK)
    gl = jnp.einsum("bshi,ghij->gbshj", xh, gate_w).reshape(2, bn, s, LRU_W)
    gl = gl.astype(jnp.float32) + gate_b.astype(jnp.float32)[:, None, None, :]
    r = jax.nn.sigmoid(gl[0])
    i = jax.nn.sigmoid(gl[1])
    log_a = -LRU_C * r * jax.nn.softplus(-lam.astype(jnp.float32))
    a = jnp.exp(log_a)
    b = jnp.sqrt(-jnp.expm1(2.0 * log_a)) * (i * xc)
    _, h = lax.associative_scan(_linear_combine, (a, b), axis=1)
    return h


def branch_rglru(a_x, a_y, conv_w, conv_b, gate_w, gate_b, lam):
    xc = dw_conv(a_x, conv_w, conv_b, CONV_PAD).astype(jnp.float32)
    h_f = rglru_scan(xc, gate_w[0], gate_b[0], lam[0])
    h_b = flip_seq(rglru_scan(flip_seq(xc), gate_w[1], gate_b[1], lam[1]))
    return (h_f + h_b) * jax.nn.gelu(a_y.astype(jnp.float32))


def gla_chunked(q, k, v, log_a):
    bn, s, h, dk = q.shape
    dv = v.shape[-1]
    nc = s // CHUNK
    q = q.reshape(bn, nc, CHUNK, h, dk)
    k = k.reshape(bn, nc, CHUNK, h, dk)
    v = v.reshape(bn, nc, CHUNK, h, dv)
    b = jnp.cumsum(log_a.reshape(bn, nc, CHUNK, h, dk), axis=2)
    b_ref = b[:, :, CHUNK // 2 - 1:CHUNK // 2]
    b_last = b[:, :, -1:]
    mask = jnp.tril(jnp.ones((CHUNK, CHUNK), dtype=bool))
    scores = jnp.einsum("bclhd,bcshd->bchls", q * jnp.exp(b - b_ref), k * jnp.exp(b_ref - b))
    scores = jnp.where(mask, scores, 0.0)
    o_intra = jnp.einsum("bchls,bcshe->bclhe", scores, v)
    local = jnp.einsum("bcshd,bcshe->bchde", k * jnp.exp(b_last - b), v)
    states = carry_states(local, jnp.exp(b_last[:, :, 0])[..., None])
    o_inter = jnp.einsum("bclhd,bchde->bclhe", q * jnp.exp(b), states)
    return (o_intra + o_inter).reshape(bn, s, h, dv)


def branch_gla(q, k, v, g, lr, wa2, ba, norm_g):
    f32 = jnp.float32
    bn, s, _ = q.shape
    q = q.astype(f32).reshape(bn, s, GLA_HEADS, GLA_DK) * (GLA_DK ** -0.5)
    k = k.astype(f32).reshape(bn, s, GLA_HEADS, GLA_DK)
    v = v.astype(f32).reshape(bn, s, GLA_HEADS, GLA_DV)
    lr = lr.astype(f32).reshape(bn, s, 2, GLA_RANK)
    logits = jnp.einsum("bsdr,drk->dbsk", lr, wa2).astype(f32) + ba.astype(f32)[:, None, None, :]
    log_a = (jax.nn.log_sigmoid(logits) / GLA_TAU).reshape(2, bn, s, GLA_HEADS, GLA_DK)
    o = gla_chunked(q, k, v, log_a[0]) + flip_seq(
        gla_chunked(flip_seq(q), flip_seq(k), flip_seq(v), flip_seq(log_a[1])))
    o = rms_norm(o, norm_g) * jax.nn.silu(g.astype(f32).reshape(bn, s, GLA_HEADS, GLA_DV))
    return o.reshape(bn, s, GLA_HEADS * GLA_DV)


def ssd_chunked(x, dt, a, bm, cm):
    bn, s, h, p = x.shape
    nc = s // CHUNK
    g = SSD_GROUPS
    r = h // g
    xdt = (x * dt[..., None]).reshape(bn, nc, CHUNK, g, r, p)
    cs = jnp.cumsum((dt * a).reshape(bn, nc, CHUNK, g, r), axis=2)
    bm = bm.reshape(bn, nc, CHUNK, g, SSD_N)
    cm = cm.reshape(bn, nc, CHUNK, g, SSD_N)
    mask = jnp.tril(jnp.ones((CHUNK, CHUNK), dtype=bool))[:, :, None, None]
    seg = cs[:, :, :, None] - cs[:, :, None, :]
    lmat = jnp.exp(jnp.where(mask, seg, -jnp.inf))
    cb = jnp.einsum("bclgn,bcsgn->bclsg", cm, bm)
    y_intra = jnp.einsum("bclsgr,bcsgrp->bclgrp", cb[..., None] * lmat, xdt)
    dec_end = jnp.exp(cs[:, :, -1:] - cs)
    local = jnp.einsum("bcsgn,bcsgrp->bcgrpn", bm, xdt * dec_end[..., None])
    states = carry_states(local, jnp.exp(cs[:, :, -1])[..., None, None])
    y_inter = jnp.einsum("bclgn,bcgrpn->bclgrp", cm, states) * jnp.exp(cs)[..., None]
    return (y_intra + y_inter).reshape(bn, s, h, p)


def branch_ssd(z, xbc, dt_raw, conv_w, conv_b, dt_bias, a_log, d_skip, norm_g):
    f32 = jnp.float32
    bn, s, _ = z.shape
    xbc = jax.nn.silu(dw_conv(xbc, conv_w, conv_b, CONV_PAD).astype(f32))
    x, bm, cm = jnp.split(xbc, [BRANCH_W, BRANCH_W + SSD_GROUPS * SSD_N], axis=-1)
    x = x.reshape(bn, s, SSD_HEADS, SSD_P)
    bm = bm.reshape(bn, s, SSD_GROUPS, SSD_N)
    cm = cm.reshape(bn, s, SSD_GROUPS, SSD_N)
    dt = jax.nn.softplus(dt_raw.astype(f32).reshape(bn, s, 2, SSD_HEADS) + dt_bias.astype(f32))
    a = -jnp.exp(a_log.astype(f32))
    y_f = ssd_chunked(x, dt[:, :, 0], a[0], bm, cm)
    y_b = flip_seq(ssd_chunked(flip_seq(x), flip_seq(dt[:, :, 1]), a[1], flip_seq(bm), flip_seq(cm)))
    y = y_f + y_b + x * d_skip.astype(f32)[:, None]
    y = y.reshape(bn, s, BRANCH_W) * jax.nn.silu(z.astype(f32))
    gw = BRANCH_W // SSD_GROUPS
    y = rms_norm(y.reshape(bn, s, SSD_GROUPS, gw), norm_g.reshape(SSD_GROUPS, gw))
    return y.reshape(bn, s, BRANCH_W)


def hyena_filters(length, w1, b1, w2, b2, w3, b3, w4, freq, decay):
    f32 = jnp.float32
    t = jnp.linspace(0.0, 1.0, length, dtype=f32)[:, None]
    w = (2.0 * math.pi / length) * jnp.arange(length, dtype=f32)[:, None]
    f = jnp.linspace(1e-4, HY_BANDS - 1, HY_BANDS, dtype=f32)[None, :]
    z = jnp.concatenate([t, jnp.cos(f * w), -jnp.sin(f * w)], axis=-1)
    h = jnp.sin(freq[0] * (z @ w1 + b1))
    h = jnp.sin(freq[1] * (h @ w2 + b2))
    h = jnp.sin(freq[2] * (h @ w3 + b3))
    h = (h @ w4).astype(f32) * jnp.exp(-t * jnp.abs(decay.astype(f32)))
    return h[:, :HY_W], h[:, HY_W:]


def two_sided_fft_conv(u, h_fwd, h_bwd, bias):
    length, ch = h_fwd.shape
    filt = jnp.concatenate(
        [h_fwd, jnp.zeros((1, ch), jnp.float32), jnp.flip(h_bwd[1:], axis=0)], axis=0)
    u_f = jnp.fft.rfft(u, n=2 * length, axis=1)
    f_f = jnp.fft.rfft(filt, n=2 * length, axis=0)
    y = jnp.fft.irfft(u_f * f_f[None], n=2 * length, axis=1)[:, :length]
    return y + u * bias.astype(jnp.float32)


def branch_hyena(d_u, conv_w, conv_b, w1, b1, w2, b2, w3, b3, w4, freq, decay, bias):
    uc = dw_conv(d_u, conv_w, conv_b, HY_SHORT_PAD).astype(jnp.float32)
    x0, x1, v = jnp.split(uc, 3, axis=-1)
    h_fwd, h_bwd = hyena_filters(uc.shape[1], w1, b1, w2, b2, w3, b3, w4, freq, decay)
    return x0 * two_sided_fft_conv(v * x1, h_fwd, h_bwd, bias)


def mixer_block(x, norm_g, w_in, lru_conv_w, lru_conv_b, lru_gate_w, lru_gate_b, lru_lambda,
                gla_wa2, gla_ba, gla_norm, ssd_conv_w, ssd_conv_b, ssd_dt_bias, ssd_a_log, ssd_d,
                ssd_norm, hy_conv_w, hy_conv_b, hy_w1, hy_b1, hy_w2, hy_b2, hy_w3, hy_b3, hy_w4,
                hy_freq, hy_decay, hy_bias, w_branch, w_out):
    bn, s, _ = x.shape
    u = rms_norm(x, norm_g) @ w_in
    (a_x, a_y, b_q, b_k, b_v, b_g, b_lr, c_z, c_xbc, c_dt, d_u, gate_logits) = jnp.split(
        u, _in_split_points(), axis=-1)
    y_a = branch_rglru(a_x, a_y, lru_conv_w, lru_conv_b, lru_gate_w, lru_gate_b, lru_lambda)
    y_b = branch_gla(b_q, b_k, b_v, b_g, b_lr, gla_wa2, gla_ba, gla_norm)
    y_c = branch_ssd(c_z, c_xbc, c_dt, ssd_conv_w, ssd_conv_b, ssd_dt_bias, ssd_a_log, ssd_d, ssd_norm)
    y_d = branch_hyena(d_u, hy_conv_w, hy_conv_b, hy_w1, hy_b1, hy_w2, hy_b2, hy_w3, hy_b3, hy_w4,
                       hy_freq, hy_decay, hy_bias)
    branches = jnp.stack([y_a, y_b, y_c, y_d], axis=2).astype(x.dtype)
    proj = jnp.einsum("bsnc,ncd->bsnd", branches, w_branch)
    gates = jax.nn.sigmoid(gate_logits.reshape(bn, s, N_BRANCH, D_MODEL))
    merged = jnp.sum(gates * proj, axis=2)
    return merged @ w_out


def cross_attention(x, mem, g_x, g_mem, wq, wkv, wo):
    bn, s, _ = x.shape
    q = (rms_norm(x, g_x) @ wq).reshape(bn, s, XA_HEADS, XA_HD)
    k, v = jnp.split(rms_norm(mem, g_mem) @ wkv, 2, axis=-1)
    k = k.reshape(bn, -1, XA_HEADS, XA_HD)
    v = v.reshape(bn, -1, XA_HEADS, XA_HD)
    scores = jnp.einsum("bqhd,bkhd->bhqk", q, k).astype(jnp.float32) * (XA_HD ** -0.5)
    p = jax.nn.softmax(scores, axis=-1).astype(v.dtype)
    o = jnp.einsum("bhqk,bkhd->bqhd", p, v).reshape(bn, s, D_MODEL)
    return o @ wo


def expert_choice_ffn(x, g, router_w, w_gate, w_up, w_down):
    bn, s, d = x.shape
    t = bn * s
    cap = max(1, EC_CAPACITY * t // N_EXPERTS)
    xt = rms_norm(x, g).reshape(t, d)
    aff = jax.nn.softmax((xt @ router_w).astype(jnp.float32), axis=-1)
    gate, idx = lax.top_k(aff.T, cap)
    xe = xt[idx]
    hdn = jax.nn.silu(jnp.einsum("ecd,edf->ecf", xe, w_gate)) * jnp.einsum("ecd,edf->ecf", xe, w_up)
    ye = jnp.einsum("ecf,efd->ecd", hdn, w_down) * gate[..., None].astype(x.dtype)
    out = jnp.zeros((t, d), x.dtype).at[idx.reshape(-1)].add(ye.reshape(-1, d))
    return out.reshape(bn, s, d)


def setup_inputs(seed: int = 0) -> dict:
    key = jax.random.key(seed)
    keys = jax.random.split(key, 64)
    cnt = [0]

    def nk():
        cnt[0] += 1
        return keys[cnt[0] - 1]

    def normal(shape, scale):
        return jax.random.normal(nk(), shape, jnp.float32) * scale

    def gain(shape):
        return 1.0 + 0.05 * jax.random.normal(nk(), shape, jnp.float32)

    def uniform(shape, lo, hi):
        return jax.random.uniform(nk(), shape, jnp.float32, lo, hi)

    L = DEPTH
    D = D_MODEL
    gk = GLA_HEADS * GLA_DK
    a0 = uniform((L, 2, LRU_W), 0.9, 0.999) ** (1.0 / LRU_C)
    dt0 = jnp.exp(uniform((L, 2, SSD_HEADS), math.log(1e-3), math.log(1e-1)))
    return {
        "x_prompt": normal((BATCH, SEQ, D), 1.0),
        "x_sample": normal((DEC_BATCH, DEC_SEQ, D), 1.0),
        "mem_prompt": normal((BATCH, MEM_LEN, D), 1.0),
        "mem_sample": normal((DEC_BATCH, MEM_LEN, D), 1.0),
        "norm_mix": gain((L, D)),
        "w_in": normal((L, D, IN_COLS), D ** -0.5),
        "lru_conv_w": normal((L, CONV_W, LRU_W), CONV_W ** -0.5),
        "lru_conv_b": normal((L, LRU_W), 0.02),
        "lru_gate_w": normal((L, 2, 2, LRU_HEADS, LRU_BLOCK, LRU_BLOCK), LRU_BLOCK ** -0.5),
        "lru_gate_b": normal((L, 2, 2, LRU_W), 0.02),
        "lru_lambda": jnp.log(a0) - jnp.log1p(-a0),
        "gla_wa2": normal((L, 2, GLA_RANK, gk), GLA_RANK ** -0.5),
        "gla_ba": normal((L, 2, gk), 0.1),
        "gla_norm": gain((L, GLA_DV)),
        "ssd_conv_w": normal((L, CONV_W, SSD_CONV_CH), CONV_W ** -0.5),
        "ssd_conv_b": normal((L, SSD_CONV_CH), 0.02),
        "ssd_dt_bias": dt0 + jnp.log(-jnp.expm1(-dt0)),
        "ssd_a_log": jnp.log(uniform((L, 2, SSD_HEADS), 1.0, 16.0)),
        "ssd_d": gain((L, SSD_HEADS)),
        "ssd_norm": gain((L, BRANCH_W)),
        "hy_conv_w": normal((L, HY_SHORT, 3 * HY_W), HY_SHORT ** -0.5),
        "hy_conv_b": normal((L, 3 * HY_W), 0.02),
        "hy_w1": normal((L, HY_EMB, HY_FH), HY_EMB ** -0.5),
        "hy_b1": normal((L, HY_FH), 0.1),
        "hy_w2": normal((L, HY_FH, HY_FH), HY_FH ** -0.5),
        "hy_b2": normal((L, HY_FH), 0.1),
        "hy_w3": normal((L, HY_FH, HY_FH), HY_FH ** -0.5),
        "hy_b3": normal((L, HY_FH), 0.1),
        "hy_w4": normal((L, HY_FH, 2 * HY_W), 0.02 * HY_FH ** -0.5),
        "hy_freq": 1.0 + 0.1 * jax.random.normal(nk(), (L, 3, HY_FH), jnp.float32),
        "hy_decay": uniform((L, 2 * HY_W), HY_MIN_DECAY, HY_MAX_DECAY),
        "hy_bias": normal((L, HY_W), 0.5),
        "w_branch": normal((L, N_BRANCH, BRANCH_W, D), BRANCH_W ** -0.5),
        "w_out": normal((L, D, D), D ** -0.5),
        "norm_xa": gain((L, D)),
        "norm_mem": gain((L, D)),
        "xa_wq": normal((L, D, D), D ** -0.5),
        "xa_wkv": normal((L, D, 2 * D), D ** -0.5),
        "xa_wo": normal((L, D, D), D ** -0.5),
        "norm_ffn": gain((L, D)),
        "router_w": normal((L, D, N_EXPERTS), D ** -0.5),
        "exp_w_gate": normal((L, N_EXPERTS, D, EXPERT_FF), D ** -0.5),
        "exp_w_up": normal((L, N_EXPERTS, D, EXPERT_FF), D ** -0.5),
        "exp_w_down": normal((L, N_EXPERTS, EXPERT_FF, D), EXPERT_FF ** -0.5),
        "final_norm": gain((D,)),
    }


def reference(x_prompt, x_sample, mem_prompt, mem_sample, norm_mix, w_in,
              lru_conv_w, lru_conv_b, lru_gate_w, lru_gate_b, lru_lambda,
              gla_wa2, gla_ba, gla_norm,
              ssd_conv_w, ssd_conv_b, ssd_dt_bias, ssd_a_log, ssd_d, ssd_norm,
              hy_conv_w, hy_conv_b, hy_w1, hy_b1, hy_w2, hy_b2, hy_w3, hy_b3, hy_w4,
              hy_freq, hy_decay, hy_bias,
              w_branch, w_out,
              norm_xa, norm_mem, xa_wq, xa_wkv, xa_wo,
              norm_ffn, router_w, exp_w_gate, exp_w_up, exp_w_down,
              final_norm):
    outs = []
    for x, mem in ((x_prompt, mem_prompt), (x_sample, mem_sample)):
        for l in range(DEPTH):
            x = x + mixer_block(
                x, norm_mix[l], w_in[l], lru_conv_w[l], lru_conv_b[l], lru_gate_w[l], lru_gate_b[l],
                lru_lambda[l], gla_wa2[l], gla_ba[l], gla_norm[l], ssd_conv_w[l], ssd_conv_b[l],
                ssd_dt_bias[l], ssd_a_log[l], ssd_d[l], ssd_norm[l], hy_conv_w[l], hy_conv_b[l],
                hy_w1[l], hy_b1[l], hy_w2[l], hy_b2[l], hy_w3[l], hy_b3[l], hy_w4[l], hy_freq[l],
                hy_decay[l], hy_bias[l], w_branch[l], w_out[l])
            x = x + cross_attention(x, mem, norm_xa[l], norm_mem[l], xa_wq[l], xa_wkv[l], xa_wo[l])
            x = x + expert_choice_ffn(x, norm_ffn[l], router_w[l], exp_w_gate[l], exp_w_up[l],
                                      exp_w_down[l])
        outs.append(rms_norm(x, final_norm))
    y_prompt, y_sample = outs
    return (y_prompt, y_sample)
```

```python
import functools
import math

import jax
import jax.numpy as jnp
import numpy as np
from jax import lax
from jax.experimental import pallas as pl
from jax.experimental.pallas import tpu as pltpu

F32 = jnp.float32
BF16 = jnp.bfloat16

EPS = 1e-6
CHUNK = 64
HALO = 8
LANES = 128
VMEM_LIMIT_BYTES = 56 * 1024 * 1024

LRU_C = 8.0
GLA_HEADS, GLA_DK, GLA_DV, GLA_RANK, GLA_TAU = 4, 64, 128, 16, 16.0
SSD_HEADS, SSD_P, SSD_GROUPS, SSD_N = 8, 64, 2, 64
XA_HEADS = 4
N_EXPERTS, EC_CAPACITY = 16, 2
HY_EMB = 33
HY_BANDS = (HY_EMB - 1) // 2


def _cparams(ndims):
    return pltpu.CompilerParams(dimension_semantics=("arbitrary",) * ndims,
                                vmem_limit_bytes=VMEM_LIMIT_BYTES)


def _rmsnorm(x, g):
    xf = x.astype(F32)
    return xf * lax.rsqrt(jnp.mean(xf * xf, axis=-1, keepdims=True) + EPS) * g


def _softplus(x):
    return jnp.maximum(x, 0.0) + jnp.log1p(jnp.exp(-jnp.abs(x)))


def _silu(x):
    return x * jax.nn.sigmoid(x)


def _gelu_tanh(x):
    return 0.5 * x * (1.0 + jnp.tanh(math.sqrt(2.0 / math.pi) * (x + 0.044715 * (x * x * x))))


def _dot(a, b):
    return jnp.dot(a.astype(BF16), b.astype(BF16), preferred_element_type=F32)


def _dot_nt(a, b):
    return lax.dot_general(a.astype(BF16), b.astype(BF16), (((1,), (1,)), ((), ())),
                           preferred_element_type=F32)


def _dot_tn(a, b):
    return lax.dot_general(a.astype(BF16), b.astype(BF16), (((0,), (0,)), ((), ())),
                           preferred_element_type=F32)


def _split_hi_lo(x):
    hi = x.astype(BF16)
    lo = (x - hi.astype(F32)).astype(BF16)
    return hi, lo


def _dot_exact_rhs(sel, x):
    hi, lo = _split_hi_lo(x)
    sel = sel.astype(BF16)
    return (jnp.dot(sel, hi, preferred_element_type=F32) + jnp.dot(sel, lo, preferred_element_type=F32))


def _dot_exact_lhs(x, sel):
    hi, lo = _split_hi_lo(x)
    sel = sel.astype(BF16)
    return (jnp.dot(hi, sel, preferred_element_type=F32) + jnp.dot(lo, sel, preferred_element_type=F32))


def _transpose_exact(x):
    n = x.shape[1]
    eye = (lax.broadcasted_iota(jnp.int32, (n, n), 0) == lax.broadcasted_iota(jnp.int32, (n, n), 1)).astype(BF16)
    hi, lo = _split_hi_lo(x)
    dn = (((1,), (1,)), ((), ()))
    return (lax.dot_general(eye, hi, dn, preferred_element_type=F32)
            + lax.dot_general(eye, lo, dn, preferred_element_type=F32))


def _tri(n, rev):
    l = lax.broadcasted_iota(jnp.int32, (n, n), 0)
    s = lax.broadcasted_iota(jnp.int32, (n, n), 1)
    return (s >= l) if rev else (s <= l)


def _conv_rows(prev, main, nxt, w_ref, b, left, first, last):
    ts = main.shape[0]
    prev = jnp.where(first, 0.0, prev)
    nxt = jnp.where(last, 0.0, nxt)
    ext = jnp.concatenate([prev, main, nxt], axis=0)
    n = ext.shape[0]
    out = b
    for j in range(w_ref.shape[0]):
        sh = (left - j) % n
        rolled = ext if sh == 0 else pltpu.roll(ext, sh, 0)
        out = out + w_ref[j:j + 1, :] * rolled[HALO:HALO + ts]
    return out


def _scan_rows(a, b, rev):
    n = a.shape[0]
    row = lax.broadcasted_iota(jnp.int32, a.shape, 0)
    s = 1
    while s < n:
        if rev:
            a_sh, b_sh, ok = pltpu.roll(a, n - s, 0), pltpu.roll(b, n - s, 0), row < n - s
        else:
            a_sh, b_sh, ok = pltpu.roll(a, s, 0), pltpu.roll(b, s, 0), row >= s
        b = b + a * jnp.where(ok, b_sh, 0.0)
        a = a * jnp.where(ok, a_sh, 1.0)
        s *= 2
    return a, b


def _inproj_kernel(x_ref, g_ref, *refs):
    n = len(refs) // 2
    xn = _rmsnorm(x_ref[...], g_ref[...]).astype(BF16)
    for w_ref, o_ref in zip(refs[:n], refs[n:]):
        o_ref[...] = jnp.dot(xn, w_ref[...], preferred_element_type=F32).astype(o_ref.dtype)


def _norm_proj(x2, g, ws, out_dtype=F32, tm=256):
    t, d = x2.shape
    tm = min(tm, t)
    in_specs = [pl.BlockSpec((tm, d), lambda i: (i, 0)), pl.BlockSpec((1, d), lambda i: (0, 0))]
    in_specs += [pl.BlockSpec(w.shape, lambda i: (0, 0)) for w in ws]
    outs = pl.pallas_call(
        _inproj_kernel,
        grid=(t // tm,),
        in_specs=in_specs,
        out_specs=[pl.BlockSpec((tm, w.shape[1]), lambda i: (i, 0)) for w in ws],
        out_shape=[jax.ShapeDtypeStruct((t, w.shape[1]), out_dtype) for w in ws],
        compiler_params=_cparams(1),
        name="norm_proj",
    )(x2, g.reshape(1, d), *ws)
    return outs


def _seq_specs(ts, nc, width):
    per = ts // HALO
    nb = nc * per

    def tile(b, d, c):
        return c + d * (nc - 1 - 2 * c)

    main = pl.BlockSpec((None, ts, width), lambda b, d, c: (b, tile(b, d, c), 0))
    prev = pl.BlockSpec((None, HALO, width), lambda b, d, c: (b, jnp.maximum(tile(b, d, c) * per - 1, 0), 0))
    nxt = pl.BlockSpec((None, HALO, width), lambda b, d, c: (b, jnp.minimum((tile(b, d, c) + 1) * per, nb - 1), 0))
    return main, prev, nxt


def _out_spec(ts, nc, width):
    return pl.BlockSpec((None, ts, width), lambda b, d, c: (b, nc - 1 - d * c, 0))


def _dir_spec(shape):
    return pl.BlockSpec((None,) + tuple(shape), lambda b, d, c: (d,) + (0,) * len(shape))


def _const_spec(shape):
    return pl.BlockSpec(tuple(shape), lambda b, d, c: (0,) * len(shape))


LRU_W = 512


def _lru_kernel(u_ref, prev_ref, next_ref, cw_ref, cb_ref, gw_ref, gb_ref, lam_ref, o_ref, hf_ref, carry_ref,
                *, ts, nc):
    d = pl.program_id(1)
    c = pl.program_id(2)

    @pl.when(c == 0)
    def _():
        carry_ref[...] = jnp.zeros_like(carry_ref)

    def run(rev):
        tile = (nc - 1 - c) if rev else c
        xc = _conv_rows(prev_ref[:, :LRU_W], u_ref[:, :LRU_W], next_ref[:, :LRU_W], cw_ref, cb_ref[...], 2,
                        tile == 0, tile == nc - 1)
        gl = _dot(xc, gw_ref[...]) + gb_ref[...]
        r = jax.nn.sigmoid(gl[:, :LRU_W])
        i = jax.nn.sigmoid(gl[:, LRU_W:])
        log_a = -LRU_C * r * _softplus(-lam_ref[...])
        a = jnp.exp(log_a)
        b = jnp.sqrt(1.0 - jnp.exp(2.0 * log_a)) * (i * xc)
        a_cum, h = _scan_rows(a, b, rev)
        h = h + a_cum * carry_ref[...]
        carry_ref[...] = h[0:1, :] if rev else h[ts - 1:ts, :]
        rows = pl.ds(pl.multiple_of(tile * ts, ts), ts)
        if rev:
            o_ref[...] = (hf_ref[rows, :] + h) * _gelu_tanh(u_ref[:, LRU_W:])
        else:
            hf_ref[rows, :] = h

    pl.when(d == 0)(functools.partial(run, False))
    pl.when(d == 1)(functools.partial(run, True))


def _lru(u_a, conv_w, conv_b, gate_w, gate_b, lam, ts):
    bn, s, width = u_a.shape
    nc = s // ts
    main, prev, nxt = _seq_specs(ts, nc, width)
    return pl.pallas_call(
        functools.partial(_lru_kernel, ts=ts, nc=nc),
        grid=(bn, 2, nc),
        in_specs=[main, prev, nxt, _const_spec(conv_w.shape), _const_spec(conv_b.shape),
                  _dir_spec(gate_w.shape[1:]), _dir_spec(gate_b.shape[1:]), _dir_spec(lam.shape[1:])],
        out_specs=_out_spec(ts, nc, LRU_W),
        out_shape=jax.ShapeDtypeStruct((bn, s, LRU_W), F32),
        scratch_shapes=[pltpu.VMEM((s, LRU_W), F32), pltpu.VMEM((1, LRU_W), F32)],
        compiler_params=_cparams(3),
        name="rglru",
    )(u_a, u_a, u_a, conv_w, conv_b, gate_w, gate_b, lam)


GLA_QK = GLA_HEADS * GLA_DK
GLA_V = GLA_HEADS * GLA_DV
GLA_COLS = 2 * GLA_QK + 2 * GLA_V + 2 * LANES


def _gla_kernel(u_ref, wa_ref, ba_ref, ng_ref, o_ref, of_ref, st_ref, *, ts, nc):
    d = pl.program_id(1)
    c = pl.program_id(2)
    nsub = ts // CHUNK

    @pl.when(c == 0)
    def _():
        st_ref[...] = jnp.zeros_like(st_ref)

    def run(rev):
        tile = (nc - 1 - c) if rev else c
        tri = _tri(CHUNK, rev)
        ref_row = CHUNK // 2 if rev else CHUNK // 2 - 1
        last_row = 0 if rev else CHUNK - 1
        code0 = 2 * GLA_QK + 2 * GLA_V + (LANES if rev else 0)

        def sub(i, carry):
            ci = (nsub - 1 - i) if rev else i
            rows = pl.ds(pl.multiple_of(ci * CHUNK, CHUNK), CHUNK)
            q = u_ref[rows, 0:GLA_QK] * (GLA_DK ** -0.5)
            k = u_ref[rows, GLA_QK:2 * GLA_QK]
            v = u_ref[rows, 2 * GLA_QK:2 * GLA_QK + GLA_V]
            logits = _dot(u_ref[rows, code0:code0 + LANES], wa_ref[...]) + ba_ref[...]
            log_a = -_softplus(-logits) * (1.0 / GLA_TAU)
            b = _dot_exact_rhs(tri, log_a)
            b_mid = b[ref_row:ref_row + 1, :]
            b_end = b[last_row:last_row + 1, :]
            qd = q * jnp.exp(b - b_mid)
            kd = k * jnp.exp(b_mid - b)
            qe = q * jnp.exp(b)
            kl = k * jnp.exp(b_end - b)
            st = st_ref[...]
            outs, locs = [], []
            for h in range(GLA_HEADS):
                ks = slice(h * GLA_DK, (h + 1) * GLA_DK)
                vh = v[:, h * GLA_DV:(h + 1) * GLA_DV]
                sc = jnp.where(tri, _dot_nt(qd[:, ks], kd[:, ks]), 0.0)
                outs.append(_dot(sc, vh) + _dot_nt(qe[:, ks], st[:, ks]))
                locs.append(_dot_tn(vh, kl[:, ks]))
            st_ref[...] = st * jnp.exp(b_end) + jnp.concatenate(locs, axis=1)
            o = jnp.concatenate(outs, axis=1)
            seq_rows = pl.ds(pl.multiple_of(tile * ts + ci * CHUNK, CHUNK), CHUNK)
            if rev:
                o = o + of_ref[seq_rows, :]
                g = u_ref[rows, 2 * GLA_QK + GLA_V:2 * GLA_QK + 2 * GLA_V]
                parts = []
                for h in range(GLA_HEADS):
                    vs = slice(h * GLA_DV, (h + 1) * GLA_DV)
                    parts.append(_rmsnorm(o[:, vs], ng_ref[...]) * _silu(g[:, vs]))
                o_ref[rows, :] = jnp.concatenate(parts, axis=1)
            else:
                of_ref[seq_rows, :] = o
            return carry

        lax.fori_loop(0, nsub, sub, 0)

    pl.when(d == 0)(functools.partial(run, False))
    pl.when(d == 1)(functools.partial(run, True))


def _gla(u_b, wa, ba, norm_g, ts):
    bn, s, width = u_b.shape
    nc = s // ts
    main, _, _ = _seq_specs(ts, nc, width)
    return pl.pallas_call(
        functools.partial(_gla_kernel, ts=ts, nc=nc),
        grid=(bn, 2, nc),
        in_specs=[main, _dir_spec(wa.shape[1:]), _dir_spec(ba.shape[1:]), _const_spec(norm_g.shape)],
        out_specs=_out_spec(ts, nc, GLA_V),
        out_shape=jax.ShapeDtypeStruct((bn, s, GLA_V), F32),
        scratch_shapes=[pltpu.VMEM((s, GLA_V), F32), pltpu.VMEM((GLA_DV, GLA_QK), F32)],
        compiler_params=_cparams(3),
        name="gla",
    )(u_b, wa, ba, norm_g)


SSD_W = 512
SSD_BC = SSD_GROUPS * SSD_N
SSD_CONV = SSD_W + 2 * SSD_BC
SSD_COLS = SSD_W + SSD_CONV + LANES
SSD_GW = SSD_W // SSD_GROUPS


def _ssd_kernel(u_ref, prev_ref, next_ref, cw_ref, cb_ref, dtb_ref, alog_ref, dsk_ref, ng_ref, o_ref,
                yf_ref, st_ref, xs_ref, *, ts, nc):
    d = pl.program_id(1)
    c = pl.program_id(2)
    nsub = ts // CHUNK
    xbc = slice(SSD_W, SSD_W + SSD_CONV)
    dtc = slice(SSD_W + SSD_CONV, SSD_COLS)

    @pl.when(c == 0)
    def _():
        st_ref[...] = jnp.zeros_like(st_ref)

    def run(rev):
        tile = (nc - 1 - c) if rev else c
        tri = _tri(CHUNK, rev)
        last_row = 0 if rev else CHUNK - 1
        xs_ref[...] = _silu(_conv_rows(prev_ref[:, xbc], u_ref[:, xbc], next_ref[:, xbc], cw_ref, cb_ref[...], 2,
                                       tile == 0, tile == nc - 1))
        lane = lax.broadcasted_iota(jnp.int32, (1, LANES), 1)
        a_row = jnp.where(lane < 2 * SSD_HEADS, -jnp.exp(alog_ref[...]), 0.0)
        er = lax.broadcasted_iota(jnp.int32, (LANES, SSD_W), 0)
        ec = lax.broadcasted_iota(jnp.int32, (LANES, SSD_W), 1)
        expand = er == (lax.shift_right_logical(ec, SSD_P.bit_length() - 1) + (SSD_HEADS if rev else 0))
        h0 = SSD_HEADS if rev else 0

        def sub(i, carry):
            ci = (nsub - 1 - i) if rev else i
            rows = pl.ds(pl.multiple_of(ci * CHUNK, CHUNK), CHUNK)
            x = xs_ref[rows, 0:SSD_W]
            bm = xs_ref[rows, SSD_W:SSD_W + SSD_BC]
            cm = xs_ref[rows, SSD_W + SSD_BC:SSD_CONV]
            dt = _softplus(u_ref[rows, dtc] + dtb_ref[...])
            cs = _dot_exact_rhs(tri, dt * a_row)
            cs_t = _transpose_exact(cs)
            both = _dot_exact_lhs(jnp.concatenate([dt, cs], axis=0), expand)
            dt_e, cs_e = both[:CHUNK], both[CHUNK:]
            cs_end = cs_e[last_row:last_row + 1, :]
            xdt = x * dt_e
            xdec = xdt * jnp.exp(cs_end - cs_e)
            grow = jnp.exp(cs_e)
            st = st_ref[...]
            ys, locs = [], []
            for g in range(SSD_GROUPS):
                ns = slice(g * SSD_N, (g + 1) * SSD_N)
                gs = slice(g * SSD_GW, (g + 1) * SSD_GW)
                cb = _dot_nt(cm[:, ns], bm[:, ns])
                y_inter = _dot(cm[:, ns], st[:, gs]) * grow[:, gs]
                intra = []
                for r in range(SSD_HEADS // SSD_GROUPS):
                    hh = g * (SSD_HEADS // SSD_GROUPS) + r
                    j = h0 + hh
                    seg = cs[:, j:j + 1] - cs_t[j:j + 1, :]
                    lmat = jnp.exp(jnp.where(tri, seg, -jnp.inf))
                    intra.append(_dot(cb * lmat, xdt[:, hh * SSD_P:(hh + 1) * SSD_P]))
                ys.append(jnp.concatenate(intra, axis=1) + y_inter)
                locs.append(_dot_tn(bm[:, ns], xdec[:, gs]))
            st_ref[...] = st * jnp.exp(cs_end) + jnp.concatenate(locs, axis=1)
            y = jnp.concatenate(ys, axis=1)
            seq_rows = pl.ds(pl.multiple_of(tile * ts + ci * CHUNK, CHUNK), CHUNK)
            if rev:
                y = y + yf_ref[seq_rows, :] + x * dsk_ref[...]
                y = y * _silu(u_ref[rows, 0:SSD_W])
                parts = []
                for g in range(SSD_GROUPS):
                    gs = slice(g * SSD_GW, (g + 1) * SSD_GW)
                    parts.append(_rmsnorm(y[:, gs], ng_ref[:, gs]))
                o_ref[rows, :] = jnp.concatenate(parts, axis=1)
            else:
                yf_ref[seq_rows, :] = y
            return carry

        lax.fori_loop(0, nsub, sub, 0)

    pl.when(d == 0)(functools.partial(run, False))
    pl.when(d == 1)(functools.partial(run, True))


def _ssd(u_c, conv_w, conv_b, dt_bias, a_log, d_skip, norm_g, ts):
    bn, s, width = u_c.shape
    nc = s // ts
    main, prev, nxt = _seq_specs(ts, nc, width)
    consts = [conv_w, conv_b, dt_bias, a_log, d_skip, norm_g]
    return pl.pallas_call(
        functools.partial(_ssd_kernel, ts=ts, nc=nc),
        grid=(bn, 2, nc),
        in_specs=[main, prev, nxt] + [_const_spec(a.shape) for a in consts],
        out_specs=_out_spec(ts, nc, SSD_W),
        out_shape=jax.ShapeDtypeStruct((bn, s, SSD_W), F32),
        scratch_shapes=[pltpu.VMEM((s, SSD_W), F32), pltpu.VMEM((SSD_N, SSD_W), F32),
                        pltpu.VMEM((ts, SSD_CONV), F32)],
        compiler_params=_cparams(3),
        name="ssd",
    )(u_c, u_c, u_c, *consts)


HY_W = 512


def _hyena_pre_kernel(u_ref, prev_ref, next_ref, cw_ref, cb_ref, x0_ref, p_ref, *, nc):
    c = pl.program_id(1)
    uc = _conv_rows(prev_ref[...], u_ref[...], next_ref[...], cw_ref, cb_ref[...], 1, c == 0, c == nc - 1)
    x0_ref[...] = uc[:, 0:HY_W]
    p_ref[...] = uc[:, 2 * HY_W:3 * HY_W] * uc[:, HY_W:2 * HY_W]


def _hyena_pre(u_d, conv_w, conv_b, ts):
    bn, s, width = u_d.shape
    nc = s // ts
    per = ts // HALO
    nb = nc * per
    main = pl.BlockSpec((None, ts, width), lambda b, c: (b, c, 0))
    prev = pl.BlockSpec((None, HALO, width), lambda b, c: (b, jnp.maximum(c * per - 1, 0), 0))
    nxt = pl.BlockSpec((None, HALO, width), lambda b, c: (b, jnp.minimum((c + 1) * per, nb - 1), 0))
    out = pl.BlockSpec((None, ts, HY_W), lambda b, c: (b, c, 0))
    return pl.pallas_call(
        functools.partial(_hyena_pre_kernel, nc=nc),
        grid=(bn, nc),
        in_specs=[main, prev, nxt, pl.BlockSpec(conv_w.shape, lambda b, c: (0, 0)),
                  pl.BlockSpec(conv_b.shape, lambda b, c: (0, 0))],
        out_specs=[out, out],
        out_shape=[jax.ShapeDtypeStruct((bn, s, HY_W), F32)] * 2,
        compiler_params=_cparams(2),
        name="hyena_pre",
    )(u_d, u_d, u_d, conv_w, conv_b)


def _hyena_filters(length, w1, b1, w2, b2, w3, b3, w4, freq, decay):
    t = jnp.linspace(0.0, 1.0, length, dtype=F32)[:, None]
    w = (2.0 * math.pi / length) * jnp.arange(length, dtype=F32)[:, None]
    f = jnp.linspace(1e-4, HY_BANDS - 1, HY_BANDS, dtype=F32)[None, :]
    z = jnp.concatenate([t, jnp.cos(f * w), -jnp.sin(f * w)], axis=-1)
    h = jnp.sin(freq[0] * (z @ w1 + b1))
    h = jnp.sin(freq[1] * (h @ w2 + b2))
    h = jnp.sin(freq[2] * (h @ w3 + b3))
    h = (h @ w4).astype(F32) * jnp.exp(-t * jnp.abs(decay.astype(F32)))
    return h[:, :HY_W], h[:, HY_W:]


def _long_conv(u, h_fwd, h_bwd, bias):
    length, ch = h_fwd.shape
    filt = jnp.concatenate([h_fwd, jnp.zeros((1, ch), F32), jnp.flip(h_bwd[1:], axis=0)], axis=0)
    u_f = jnp.fft.rfft(u, n=2 * length, axis=1)
    f_f = jnp.fft.rfft(filt, n=2 * length, axis=0)
    y = jnp.fft.irfft(u_f * f_f[None], n=2 * length, axis=1)[:, :length]
    return y + u * bias.astype(F32)


def _merge_kernel(x_ref, g_ref, ya_ref, yb_ref, yc_ref, yd_ref, wg_ref, wb_ref, wo_ref, o_ref):
    x = x_ref[...]
    d = x.shape[1]
    xn = _rmsnorm(x, g_ref[...]).astype(BF16)
    merged = None
    for n, y_ref in enumerate((ya_ref, yb_ref, yc_ref, yd_ref)):
        gate = jax.nn.sigmoid(jnp.dot(xn, wg_ref[:, n * d:(n + 1) * d], preferred_element_type=F32))
        term = gate * _dot(y_ref[...], wb_ref[n])
        merged = term if merged is None else merged + term
    o_ref[...] = x + _dot(merged, wo_ref[...])


def _merge(x2, g, ys, w_gate, w_branch, w_out, tm=256):
    t, d = x2.shape
    tm = min(tm, t)
    row = lambda w: pl.BlockSpec((tm, w), lambda i: (i, 0))
    full = lambda a: pl.BlockSpec(a.shape, lambda i: (0,) * a.ndim)
    return pl.pallas_call(
        _merge_kernel,
        grid=(t // tm,),
        in_specs=[row(d), pl.BlockSpec((1, d), lambda i: (0, 0))] + [row(y.shape[1]) for y in ys]
        + [full(w_gate), full(w_branch), full(w_out)],
        out_specs=row(d),
        out_shape=jax.ShapeDtypeStruct((t, d), F32),
        compiler_params=_cparams(1),
        name="merge",
    )(x2, g.reshape(1, d), *ys, w_gate, w_branch, w_out)


def _xattn_kernel(x_ref, g_ref, kv_ref, wq_ref, wo_ref, o_ref):
    x = x_ref[...]
    d = x.shape[1]
    hd = d // XA_HEADS
    q = _dot(_rmsnorm(x, g_ref[...]), wq_ref[...])
    outs = []
    for h in range(XA_HEADS):
        k = kv_ref[:, h * hd:(h + 1) * hd]
        v = kv_ref[:, d + h * hd:d + (h + 1) * hd]
        s = _dot_nt(q[:, h * hd:(h + 1) * hd], k) * (hd ** -0.5)
        s = s - jnp.max(s, axis=-1, keepdims=True)
        e = jnp.exp(s)
        p = e / jnp.sum(e, axis=-1, keepdims=True)
        outs.append(_dot(p, v))
    o_ref[...] = x + _dot(jnp.concatenate(outs, axis=1), wo_ref[...])


def _xattn(x3, g, kv, wq, wo, tq=512):
    bn, s, d = x3.shape
    m = kv.shape[1]
    tq = min(tq, s)
    return pl.pallas_call(
        _xattn_kernel,
        grid=(bn, s // tq),
        in_specs=[pl.BlockSpec((None, tq, d), lambda b, i: (b, i, 0)),
                  pl.BlockSpec((1, d), lambda b, i: (0, 0)),
                  pl.BlockSpec((None, m, 2 * d), lambda b, i: (b, 0, 0)),
                  pl.BlockSpec(wq.shape, lambda b, i: (0, 0)),
                  pl.BlockSpec(wo.shape, lambda b, i: (0, 0))],
        out_specs=pl.BlockSpec((None, tq, d), lambda b, i: (b, i, 0)),
        out_shape=jax.ShapeDtypeStruct((bn, s, d), F32),
        compiler_params=_cparams(2),
        name="xattn",
    )(x3, g.reshape(1, d), kv, wq, wo)


def _router_kernel(x_ref, g_ref, wr_ref, aff_ref, xt_ref):
    xn = _rmsnorm(x_ref[...], g_ref[...])
    xt_ref[...] = xn.astype(xt_ref.dtype)
    logits = jnp.dot(xn.astype(BF16), wr_ref[...], preferred_element_type=F32)
    lane = lax.broadcasted_iota(jnp.int32, logits.shape, 1)
    logits = jnp.where(lane < N_EXPERTS, logits, -jnp.inf)
    e = jnp.exp(logits - jnp.max(logits, axis=-1, keepdims=True))
    aff_ref[...] = e / jnp.sum(e, axis=-1, keepdims=True)


def _router(x2, g, wr_pad, tm=512):
    t, d = x2.shape
    tm = min(tm, t)
    return pl.pallas_call(
        _router_kernel,
        grid=(t // tm,),
        in_specs=[pl.BlockSpec((tm, d), lambda i: (i, 0)), pl.BlockSpec((1, d), lambda i: (0, 0)),
                  pl.BlockSpec(wr_pad.shape, lambda i: (0, 0))],
        out_specs=[pl.BlockSpec((tm, LANES), lambda i: (i, 0)), pl.BlockSpec((tm, d), lambda i: (i, 0))],
        out_shape=[jax.ShapeDtypeStruct((t, LANES), F32), jax.ShapeDtypeStruct((t, d), BF16)],
        compiler_params=_cparams(1),
        name="router",
    )(x2, g.reshape(1, d), wr_pad)


def _expert_kernel(xe_ref, gate_ref, wg_ref, wu_ref, wd_ref, o_ref):
    xe = xe_ref[...]
    hdn = _silu(jnp.dot(xe, wg_ref[...], preferred_element_type=F32)) * jnp.dot(
        xe, wu_ref[...], preferred_element_type=F32)
    o_ref[...] = _dot(hdn, wd_ref[...]) * gate_ref[...]


def _experts(xe, gate, w_gate, w_up, w_down, tm=512):
    ne, cap, d = xe.shape
    ff = w_gate.shape[2]
    tm = min(tm, cap)
    wspec = lambda a: pl.BlockSpec((None,) + a.shape[1:], lambda e, i: (e, 0, 0))
    return pl.pallas_call(
        _expert_kernel,
        grid=(ne, cap // tm),
        in_specs=[pl.BlockSpec((None, tm, d), lambda e, i: (e, i, 0)),
                  pl.BlockSpec((None, tm, 1), lambda e, i: (e, i, 0)),
                  wspec(w_gate), wspec(w_up), wspec(w_down)],
        out_specs=pl.BlockSpec((None, tm, d), lambda e, i: (e, i, 0)),
        out_shape=jax.ShapeDtypeStruct((ne, cap, d), F32),
        compiler_params=_cparams(2),
        name="experts",
    )(xe, gate[..., None], w_gate, w_up, w_down)


def _final_norm_kernel(x_ref, g_ref, o_ref):
    o_ref[...] = _rmsnorm(x_ref[...], g_ref[...])


def _final_norm(x2, g, tm=512):
    t, d = x2.shape
    tm = min(tm, t)
    return pl.pallas_call(
        _final_norm_kernel,
        grid=(t // tm,),
        in_specs=[pl.BlockSpec((tm, d), lambda i: (i, 0)), pl.BlockSpec((1, d), lambda i: (0, 0))],
        out_specs=pl.BlockSpec((tm, d), lambda i: (i, 0)),
        out_shape=jax.ShapeDtypeStruct((t, d), F32),
        compiler_params=_cparams(1),
        name="final_norm",
    )(x2, g.reshape(1, d))


def _pad_cols(w, n):
    return jnp.pad(w, ((0, 0), (0, n - w.shape[1])))


def _block_diag(w):
    h, blk, _ = w.shape
    eye = jnp.eye(h, dtype=w.dtype)
    return jnp.einsum("hij,hk->hikj", w, eye).reshape(h * blk, h * blk)


def _row128(v):
    return _pad_cols(v.reshape(1, -1), LANES)


def _prep_layer(p, l):
    d = p["w_in"].shape[1]
    w_in = p["w_in"][l]
    sizes = (512, 512, 256, 256, 512, 512, 32, 512, 768, 16, 1536, 4 * d)
    offs = np.concatenate([[0], np.cumsum(sizes)])
    seg = [w_in[:, offs[i]:offs[i + 1]] for i in range(len(sizes))]
    a_x, a_y, b_q, b_k, b_v, b_g, b_lr, c_z, c_xbc, c_dt, d_u, gates = seg
    w_a = jnp.concatenate([a_x, a_y], axis=1)
    w_b = jnp.concatenate([b_q, b_k, b_v, b_g, _pad_cols(b_lr[:, :GLA_RANK], LANES),
                           _pad_cols(b_lr[:, GLA_RANK:], LANES)], axis=1)
    w_c = jnp.concatenate([c_z, c_xbc, _pad_cols(c_dt, LANES)], axis=1)
    out = {
        "w_mix": [w.astype(BF16) for w in (w_a, w_b, w_c, d_u)],
        "w_gates": gates.astype(BF16),
        "lru_gate_w": jnp.stack([jnp.concatenate([_block_diag(p["lru_gate_w"][l, dd, 0]),
                                                  _block_diag(p["lru_gate_w"][l, dd, 1])], axis=1)
                                 for dd in range(2)]).astype(BF16),
        "lru_gate_b": p["lru_gate_b"][l].reshape(2, 1, 2 * LRU_W),
        "lru_lambda": p["lru_lambda"][l].reshape(2, 1, LRU_W),
        "lru_conv_w": p["lru_conv_w"][l],
        "lru_conv_b": p["lru_conv_b"][l].reshape(1, -1),
        "gla_wa": jnp.pad(p["gla_wa2"][l], ((0, 0), (0, LANES - GLA_RANK), (0, 0))).astype(BF16),
        "gla_ba": p["gla_ba"][l].reshape(2, 1, GLA_QK),
        "gla_norm": p["gla_norm"][l].reshape(1, GLA_DV),
        "ssd_conv_w": p["ssd_conv_w"][l],
        "ssd_conv_b": p["ssd_conv_b"][l].reshape(1, -1),
        "ssd_dt_bias": _row128(p["ssd_dt_bias"][l]),
        "ssd_a_log": _row128(p["ssd_a_log"][l]),
        "ssd_d": jnp.repeat(p["ssd_d"][l], SSD_P).reshape(1, SSD_W),
        "ssd_norm": p["ssd_norm"][l].reshape(1, SSD_W),
        "hy_conv_w": p["hy_conv_w"][l],
        "hy_conv_b": p["hy_conv_b"][l].reshape(1, -1),
        "w_branch": p["w_branch"][l].astype(BF16),
        "w_out": p["w_out"][l].astype(BF16),
        "xa_wq": p["xa_wq"][l].astype(BF16),
        "xa_wkv": p["xa_wkv"][l].astype(BF16),
        "xa_wo": p["xa_wo"][l].astype(BF16),
        "router_w": _pad_cols(p["router_w"][l], LANES).astype(BF16),
        "exp_w_gate": p["exp_w_gate"][l].astype(BF16),
        "exp_w_up": p["exp_w_up"][l].astype(BF16),
        "exp_w_down": p["exp_w_down"][l].astype(BF16),
    }
    out.update({k: p[k][l] for k in RAW_KEYS})
    return out


RAW_KEYS = ("norm_mix", "norm_xa", "norm_mem", "norm_ffn", "hy_w1", "hy_b1", "hy_w2", "hy_b2", "hy_w3", "hy_b3",
            "hy_w4", "hy_freq", "hy_decay", "hy_bias")


def _seq_tile(s):
    return min(256, s)


def _hyena(u_d, q, ts):
    s = u_d.shape[1]
    x0, pg = _hyena_pre(u_d, q["hy_conv_w"], q["hy_conv_b"], ts)
    h_fwd, h_bwd = _hyena_filters(s, q["hy_w1"], q["hy_b1"], q["hy_w2"], q["hy_b2"], q["hy_w3"], q["hy_b3"],
                                  q["hy_w4"], q["hy_freq"], q["hy_decay"])
    return x0 * _long_conv(pg, h_fwd, h_bwd, q["hy_bias"])


def _xattn_layer(x3, mem, q):
    bn, m, d = mem.shape
    (kv,) = _norm_proj(mem.reshape(bn * m, d), q["norm_mem"], [q["xa_wkv"]], out_dtype=BF16)
    return _xattn(x3, q["norm_xa"], kv.reshape(bn, m, 2 * d), q["xa_wq"], q["xa_wo"])


def _moe_layer(x2, q):
    t, d = x2.shape
    cap = max(1, EC_CAPACITY * t // N_EXPERTS)
    aff, xt = _router(x2, q["norm_ffn"], q["router_w"])
    gate, idx = lax.top_k(aff[:, :N_EXPERTS].T, cap)
    ye = _experts(xt[idx], gate, q["exp_w_gate"], q["exp_w_up"], q["exp_w_down"])
    return x2.at[idx.reshape(-1)].add(ye.reshape(-1, d))


def _layer(x, mem, q):
    bn, s, d = x.shape
    t = bn * s
    ts = _seq_tile(s)
    x2 = x.reshape(t, d)

    u_a, u_b, u_c, u_d = _norm_proj(x2, q["norm_mix"], q["w_mix"])
    y_a = _lru(u_a.reshape(bn, s, -1), q["lru_conv_w"], q["lru_conv_b"], q["lru_gate_w"], q["lru_gate_b"],
               q["lru_lambda"], ts)
    y_b = _gla(u_b.reshape(bn, s, -1), q["gla_wa"], q["gla_ba"], q["gla_norm"], ts)
    y_c = _ssd(u_c.reshape(bn, s, -1), q["ssd_conv_w"], q["ssd_conv_b"], q["ssd_dt_bias"], q["ssd_a_log"],
               q["ssd_d"], q["ssd_norm"], ts)
    y_d = _hyena(u_d.reshape(bn, s, -1), q, ts)
    ys = [y.reshape(t, -1) for y in (y_a, y_b, y_c, y_d)]
    x2 = _merge(x2, q["norm_mix"], ys, q["w_gates"], q["w_branch"], q["w_out"])

    x3 = _xattn_layer(x2.reshape(bn, s, d), mem, q)

    return _moe_layer(x3.reshape(t, d), q).reshape(bn, s, d)


def kernel(x_prompt, x_sample, mem_prompt, mem_sample, norm_mix, w_in, lru_conv_w, lru_conv_b, lru_gate_w, lru_gate_b, lru_lambda, gla_wa2, gla_ba, gla_norm, ssd_conv_w, ssd_conv_b, ssd_dt_bias, ssd_a_log, ssd_d, ssd_norm, hy_conv_w, hy_conv_b, hy_w1, hy_b1, hy_w2, hy_b2, hy_w3, hy_b3, hy_w4, hy_freq, hy_decay, hy_bias, w_branch, w_out, norm_xa, norm_mem, xa_wq, xa_wkv, xa_wo, norm_ffn, router_w, exp_w_gate, exp_w_up, exp_w_down, final_norm):
    p = dict(norm_mix=norm_mix, w_in=w_in, lru_conv_w=lru_conv_w, lru_conv_b=lru_conv_b, lru_gate_w=lru_gate_w,
             lru_gate_b=lru_gate_b, lru_lambda=lru_lambda, gla_wa2=gla_wa2, gla_ba=gla_ba, gla_norm=gla_norm,
             ssd_conv_w=ssd_conv_w, ssd_conv_b=ssd_conv_b, ssd_dt_bias=ssd_dt_bias, ssd_a_log=ssd_a_log,
             ssd_d=ssd_d, ssd_norm=ssd_norm, hy_conv_w=hy_conv_w, hy_conv_b=hy_conv_b, hy_w1=hy_w1, hy_b1=hy_b1,
             hy_w2=hy_w2, hy_b2=hy_b2, hy_w3=hy_w3, hy_b3=hy_b3, hy_w4=hy_w4, hy_freq=hy_freq, hy_decay=hy_decay,
             hy_bias=hy_bias, w_branch=w_branch, w_out=w_out, norm_xa=norm_xa, norm_mem=norm_mem, xa_wq=xa_wq,
             xa_wkv=xa_wkv, xa_wo=xa_wo, norm_ffn=norm_ffn, router_w=router_w, exp_w_gate=exp_w_gate,
             exp_w_up=exp_w_up, exp_w_down=exp_w_down)
    depth = w_in.shape[0]
    stacked = jax.tree.map(lambda *a: jnp.stack(a), *[_prep_layer(p, l) for l in range(depth)])
    outs = []
    for x, mem in ((x_prompt, mem_prompt), (x_sample, mem_sample)):
        x, _ = lax.scan(lambda xc, q, mem=mem: (_layer(xc, mem, q), None), x, stacked)
        bn, s, d = x.shape
        outs.append(_final_norm(x.reshape(bn * s, d), final_norm).reshape(bn, s, d))
    return tuple(outs)
```

```python
import functools
import math

import jax
import jax.numpy as jnp
import numpy as np
from jax import lax
from jax.experimental import pallas as pl
from jax.experimental.pallas import tpu as pltpu

F32 = jnp.float32
BF16 = jnp.bfloat16

EPS = 1e-6
CHUNK = 64
HALO = 8
LANES = 128
VMEM_LIMIT_BYTES = 56 * 1024 * 1024

LRU_C = 8.0
GLA_HEADS, GLA_DK, GLA_DV, GLA_RANK, GLA_TAU = 4, 64, 128, 16, 16.0
SSD_HEADS, SSD_P, SSD_GROUPS, SSD_N = 8, 64, 2, 64
XA_HEADS = 4
N_EXPERTS, EC_CAPACITY = 16, 2
HY_EMB = 33
HY_BANDS = (HY_EMB - 1) // 2


def _cparams(ndims):
    return pltpu.CompilerParams(dimension_semantics=("arbitrary",) * ndims,
                                vmem_limit_bytes=VMEM_LIMIT_BYTES)


def _rmsnorm(x, g):
    xf = x.astype(F32)
    return xf * lax.rsqrt(jnp.mean(xf * xf, axis=-1, keepdims=True) + EPS) * g


def _softplus(x):
    return jnp.maximum(x, 0.0) + jnp.log1p(jnp.exp(-jnp.abs(x)))


def _silu(x):
    return x * jax.nn.sigmoid(x)


def _gelu_tanh(x):
    return 0.5 * x * (1.0 + jnp.tanh(math.sqrt(2.0 / math.pi) * (x + 0.044715 * (x * x * x))))


def _dot(a, b):
    return jnp.dot(a.astype(BF16), b.astype(BF16), preferred_element_type=F32)


def _dot_nt(a, b):
    return lax.dot_general(a.astype(BF16), b.astype(BF16), (((1,), (1,)), ((), ())),
                           preferred_element_type=F32)


def _dot_tn(a, b):
    return lax.dot_general(a.astype(BF16), b.astype(BF16), (((0,), (0,)), ((), ())),
                           preferred_element_type=F32)


def _split_hi_lo(x):
    hi = x.astype(BF16)
    lo = (x - hi.astype(F32)).astype(BF16)
    return hi, lo


def _dot_exact_rhs(sel, x):
    hi, lo = _split_hi_lo(x)
    sel = sel.astype(BF16)
    return (jnp.dot(sel, hi, preferred_element_type=F32) + jnp.dot(sel, lo, preferred_element_type=F32))


def _dot_exact_lhs(x, sel):
    hi, lo = _split_hi_lo(x)
    sel = sel.astype(BF16)
    return (jnp.dot(hi, sel, preferred_element_type=F32) + jnp.dot(lo, sel, preferred_element_type=F32))


def _transpose_exact(x):
    n = x.shape[1]
    eye = (lax.broadcasted_iota(jnp.int32, (n, n), 0) == lax.broadcasted_iota(jnp.int32, (n, n), 1)).astype(BF16)
    hi, lo = _split_hi_lo(x)
    dn = (((1,), (1,)), ((), ()))
    return (lax.dot_general(eye, hi, dn, preferred_element_type=F32)
            + lax.dot_general(eye, lo, dn, preferred_element_type=F32))


def _tri(n, rev):
    l = lax.broadcasted_iota(jnp.int32, (n, n), 0)
    s = lax.broadcasted_iota(jnp.int32, (n, n), 1)
    return (s >= l) if rev else (s <= l)


def _conv_rows(prev, main, nxt, w_ref, b, left, first, last):
    ts = main.shape[0]
    prev = jnp.where(first, 0.0, prev)
    nxt = jnp.where(last, 0.0, nxt)
    ext = jnp.concatenate([prev, main, nxt], axis=0)
    n = ext.shape[0]
    out = b
    for j in range(w_ref.shape[0]):
        sh = (left - j) % n
        rolled = ext if sh == 0 else pltpu.roll(ext, sh, 0)
        out = out + w_ref[j:j + 1, :] * rolled[HALO:HALO + ts]
    return out


def _scan_rows(a, b, rev):
    n = a.shape[0]
    row = lax.broadcasted_iota(jnp.int32, a.shape, 0)
    s = 1
    while s < n:
        if rev:
            a_sh, b_sh, ok = pltpu.roll(a, n - s, 0), pltpu.roll(b, n - s, 0), row < n - s
        else:
            a_sh, b_sh, ok = pltpu.roll(a, s, 0), pltpu.roll(b, s, 0), row >= s
        b = b + a * jnp.where(ok, b_sh, 0.0)
        a = a * jnp.where(ok, a_sh, 1.0)
        s *= 2
    return a, b


def _inproj_kernel(x_ref, g_ref, *refs):
    n = len(refs) // 2
    xn = _rmsnorm(x_ref[...], g_ref[...]).astype(BF16)
    for w_ref, o_ref in zip(refs[:n], refs[n:]):
        o_ref[...] = jnp.dot(xn, w_ref[...], preferred_element_type=F32).astype(o_ref.dtype)


def _norm_proj(x2, g, ws, out_dtype=F32, tm=256):
    t, d = x2.shape
    tm = min(tm, t)
    in_specs = [pl.BlockSpec((tm, d), lambda i: (i, 0)), pl.BlockSpec((1, d), lambda i: (0, 0))]
    in_specs += [pl.BlockSpec(w.shape, lambda i: (0, 0)) for w in ws]
    outs = pl.pallas_call(
        _inproj_kernel,
        grid=(t // tm,),
        in_specs=in_specs,
        out_specs=[pl.BlockSpec((tm, w.shape[1]), lambda i: (i, 0)) for w in ws],
        out_shape=[jax.ShapeDtypeStruct((t, w.shape[1]), out_dtype) for w in ws],
        compiler_params=_cparams(1),
        name="norm_proj",
    )(x2, g.reshape(1, d), *ws)
    return outs


def _seq_specs(ts, nc, width):
    per = ts // HALO
    nb = nc * per

    def tile(b, d, c):
        return c + d * (nc - 1 - 2 * c)

    main = pl.BlockSpec((None, ts, width), lambda b, d, c: (b, tile(b, d, c), 0))
    prev = pl.BlockSpec((None, HALO, width), lambda b, d, c: (b, jnp.maximum(tile(b, d, c) * per - 1, 0), 0))
    nxt = pl.BlockSpec((None, HALO, width), lambda b, d, c: (b, jnp.minimum((tile(b, d, c) + 1) * per, nb - 1), 0))
    return main, prev, nxt


def _out_spec(ts, nc, width):
    return pl.BlockSpec((None, ts, width), lambda b, d, c: (b, nc - 1 - d * c, 0))


def _dir_spec(shape):
    return pl.BlockSpec((None,) + tuple(shape), lambda b, d, c: (d,) + (0,) * len(shape))


def _const_spec(shape):
    return pl.BlockSpec(tuple(shape), lambda b, d, c: (0,) * len(shape))


LRU_W = 512


def _lru_kernel(u_ref, prev_ref, next_ref, cw_ref, cb_ref, gw_ref, gb_ref, lam_ref, o_ref, hf_ref, carry_ref,
                *, ts, nc):
    d = pl.program_id(1)
    c = pl.program_id(2)

    @pl.when(c == 0)
    def _():
        carry_ref[...] = jnp.zeros_like(carry_ref)

    def run(rev):
        tile = (nc - 1 - c) if rev else c
        xc = _conv_rows(prev_ref[:, :LRU_W], u_ref[:, :LRU_W], next_ref[:, :LRU_W], cw_ref, cb_ref[...], 2,
                        tile == 0, tile == nc - 1)
        gl = _dot(xc, gw_ref[...]) + gb_ref[...]
        r = jax.nn.sigmoid(gl[:, :LRU_W])
        i = jax.nn.sigmoid(gl[:, LRU_W:])
        log_a = -LRU_C * r * _softplus(-lam_ref[...])
        a = jnp.exp(log_a)
        b = jnp.sqrt(1.0 - jnp.exp(2.0 * log_a)) * (i * xc)
        a_cum, h = _scan_rows(a, b, rev)
        h = h + a_cum * carry_ref[...]
        carry_ref[...] = h[0:1, :] if rev else h[ts - 1:ts, :]
        rows = pl.ds(pl.multiple_of(tile * ts, ts), ts)
        if rev:
            o_ref[...] = (hf_ref[rows, :] + h) * _gelu_tanh(u_ref[:, LRU_W:])
        else:
            hf_ref[rows, :] = h

    pl.when(d == 0)(functools.partial(run, False))
    pl.when(d == 1)(functools.partial(run, True))


def _lru(u_a, conv_w, conv_b, gate_w, gate_b, lam, ts):
    bn, s, width = u_a.shape
    nc = s // ts
    main, prev, nxt = _seq_specs(ts, nc, width)
    return pl.pallas_call(
        functools.partial(_lru_kernel, ts=ts, nc=nc),
        grid=(bn, 2, nc),
        in_specs=[main, prev, nxt, _const_spec(conv_w.shape), _const_spec(conv_b.shape),
                  _dir_spec(gate_w.shape[1:]), _dir_spec(gate_b.shape[1:]), _dir_spec(lam.shape[1:])],
        out_specs=_out_spec(ts, nc, LRU_W),
        out_shape=jax.ShapeDtypeStruct((bn, s, LRU_W), F32),
        scratch_shapes=[pltpu.VMEM((s, LRU_W), F32), pltpu.VMEM((1, LRU_W), F32)],
        compiler_params=_cparams(3),
        name="rglru",
    )(u_a, u_a, u_a, conv_w, conv_b, gate_w, gate_b, lam)


GLA_QK = GLA_HEADS * GLA_DK
GLA_V = GLA_HEADS * GLA_DV
GLA_COLS = 2 * GLA_QK + 2 * GLA_V + 2 * LANES


def _gla_kernel(u_ref, wa_ref, ba_ref, ng_ref, o_ref, of_ref, st_ref, *, ts, nc):
    d = pl.program_id(1)
    c = pl.program_id(2)
    nsub = ts // CHUNK

    @pl.when(c == 0)
    def _():
        st_ref[...] = jnp.zeros_like(st_ref)

    def run(rev):
        tile = (nc - 1 - c) if rev else c
        tri = _tri(CHUNK, rev)
        ref_row = CHUNK // 2 if rev else CHUNK // 2 - 1
        last_row = 0 if rev else CHUNK - 1
        code0 = 2 * GLA_QK + 2 * GLA_V + (LANES if rev else 0)
        stacked = (GLA_HEADS * CHUNK, GLA_QK)
        head_mask = (lax.shift_right_logical(lax.broadcasted_iota(jnp.int32, stacked, 0), CHUNK.bit_length() - 1)
                     == lax.shift_right_logical(lax.broadcasted_iota(jnp.int32, stacked, 1), GLA_DK.bit_length() - 1))
        tri_heads = jnp.concatenate([tri] * GLA_HEADS, axis=0)

        stack = lambda x: jnp.where(head_mask, jnp.concatenate([x] * GLA_HEADS, axis=0), 0.0)
        order = list(range(nsub - 1, -1, -1)) if rev else list(range(nsub))

        logits = _dot(u_ref[:, code0:code0 + LANES], wa_ref[...]) + ba_ref[...]
        log_a = -_softplus(-logits) * (1.0 / GLA_TAU)
        work = {}
        for ci in order:
            rows = slice(ci * CHUNK, (ci + 1) * CHUNK)
            q = u_ref[rows, 0:GLA_QK] * (GLA_DK ** -0.5)
            k = u_ref[rows, GLA_QK:2 * GLA_QK]
            v = u_ref[rows, 2 * GLA_QK:2 * GLA_QK + GLA_V]
            b = _dot_exact_rhs(tri, log_a[rows])
            b_mid = b[ref_row:ref_row + 1, :]
            b_end = b[last_row:last_row + 1, :]
            sc = jnp.where(tri_heads, _dot_nt(stack(q * jnp.exp(b - b_mid)), k * jnp.exp(b_mid - b)), 0.0)
            o_intra = _dot(sc, v)
            v_heads = jnp.concatenate([v[:, h * GLA_DV:(h + 1) * GLA_DV] for h in range(GLA_HEADS)], axis=0)
            loc = _dot_tn(v_heads, stack(k * jnp.exp(b_end - b)))
            work[ci] = (o_intra, stack(q * jnp.exp(b)).astype(BF16), jnp.exp(b_end), loc)

        st = st_ref[...]
        outs = {}
        for ci in order:
            o_intra, qe, decay, loc = work[ci]
            o_inter = _dot_nt(qe, st)
            outs[ci] = jnp.concatenate(
                [o_intra[h * CHUNK:(h + 1) * CHUNK, h * GLA_DV:(h + 1) * GLA_DV] + o_inter[h * CHUNK:(h + 1) * CHUNK]
                 for h in range(GLA_HEADS)], axis=1)
            st = st * decay + loc
        st_ref[...] = st

        seq_rows = pl.ds(pl.multiple_of(tile * ts, ts), ts)
        o = jnp.concatenate([outs[ci] for ci in range(nsub)], axis=0)
        if rev:
            o = o + of_ref[seq_rows, :]
            g = u_ref[:, 2 * GLA_QK + GLA_V:2 * GLA_QK + 2 * GLA_V]
            parts = []
            for h in range(GLA_HEADS):
                vs = slice(h * GLA_DV, (h + 1) * GLA_DV)
                parts.append(_rmsnorm(o[:, vs], ng_ref[...]) * _silu(g[:, vs]))
            o_ref[...] = jnp.concatenate(parts, axis=1)
        else:
            of_ref[seq_rows, :] = o

    pl.when(d == 0)(functools.partial(run, False))
    pl.when(d == 1)(functools.partial(run, True))


def _gla(u_b, wa, ba, norm_g, ts):
    bn, s, width = u_b.shape
    nc = s // ts
    main, _, _ = _seq_specs(ts, nc, width)
    return pl.pallas_call(
        functools.partial(_gla_kernel, ts=ts, nc=nc),
        grid=(bn, 2, nc),
        in_specs=[main, _dir_spec(wa.shape[1:]), _dir_spec(ba.shape[1:]), _const_spec(norm_g.shape)],
        out_specs=_out_spec(ts, nc, GLA_V),
        out_shape=jax.ShapeDtypeStruct((bn, s, GLA_V), F32),
        scratch_shapes=[pltpu.VMEM((s, GLA_V), F32), pltpu.VMEM((GLA_DV, GLA_QK), F32)],
        compiler_params=_cparams(3),
        name="gla",
    )(u_b, wa, ba, norm_g)


SSD_W = 512
SSD_BC = SSD_GROUPS * SSD_N
SSD_CONV = SSD_W + 2 * SSD_BC
SSD_COLS = SSD_W + SSD_CONV + LANES
SSD_GW = SSD_W // SSD_GROUPS


def _ssd_kernel(u_ref, prev_ref, next_ref, cw_ref, cb_ref, dtb_ref, alog_ref, dsk_ref, ng_ref, o_ref,
                yf_ref, st_ref, xs_ref, *, ts, nc):
    d = pl.program_id(1)
    c = pl.program_id(2)
    nsub = ts // CHUNK
    xbc = slice(SSD_W, SSD_W + SSD_CONV)
    dtc = slice(SSD_W + SSD_CONV, SSD_COLS)

    @pl.when(c == 0)
    def _():
        st_ref[...] = jnp.zeros_like(st_ref)

    def run(rev):
        tile = (nc - 1 - c) if rev else c
        tri = _tri(CHUNK, rev)
        last_row = 0 if rev else CHUNK - 1
        xs_ref[...] = _silu(_conv_rows(prev_ref[:, xbc], u_ref[:, xbc], next_ref[:, xbc], cw_ref, cb_ref[...], 2,
                                       tile == 0, tile == nc - 1))
        lane = lax.broadcasted_iota(jnp.int32, (1, LANES), 1)
        a_row = jnp.where(lane < 2 * SSD_HEADS, -jnp.exp(alog_ref[...]), 0.0)
        er = lax.broadcasted_iota(jnp.int32, (LANES, SSD_W), 0)
        ec = lax.broadcasted_iota(jnp.int32, (LANES, SSD_W), 1)
        expand = er == (lax.shift_right_logical(ec, SSD_P.bit_length() - 1) + (SSD_HEADS if rev else 0))
        h0 = SSD_HEADS if rev else 0

        def sub(i, carry):
            ci = (nsub - 1 - i) if rev else i
            rows = pl.ds(pl.multiple_of(ci * CHUNK, CHUNK), CHUNK)
            x = xs_ref[rows, 0:SSD_W]
            bm = xs_ref[rows, SSD_W:SSD_W + SSD_BC]
            cm = xs_ref[rows, SSD_W + SSD_BC:SSD_CONV]
            dt = _softplus(u_ref[rows, dtc] + dtb_ref[...])
            cs = _dot_exact_rhs(tri, dt * a_row)
            cs_t = _transpose_exact(cs)
            both = _dot_exact_lhs(jnp.concatenate([dt, cs], axis=0), expand)
            dt_e, cs_e = both[:CHUNK], both[CHUNK:]
            cs_end = cs_e[last_row:last_row + 1, :]
            xdt = x * dt_e
            xdec = xdt * jnp.exp(cs_end - cs_e)
            grow = jnp.exp(cs_e)
            st = st_ref[...]
            ys, locs = [], []
            for g in range(SSD_GROUPS):
                ns = slice(g * SSD_N, (g + 1) * SSD_N)
                gs = slice(g * SSD_GW, (g + 1) * SSD_GW)
                cb = _dot_nt(cm[:, ns], bm[:, ns])
                y_inter = _dot(cm[:, ns], st[:, gs]) * grow[:, gs]
                intra = []
                for r in range(SSD_HEADS // SSD_GROUPS):
                    hh = g * (SSD_HEADS // SSD_GROUPS) + r
                    j = h0 + hh
                    seg = cs[:, j:j + 1] - cs_t[j:j + 1, :]
                    lmat = jnp.exp(jnp.where(tri, seg, -jnp.inf))
                    intra.append(_dot(cb * lmat, xdt[:, hh * SSD_P:(hh + 1) * SSD_P]))
                ys.append(jnp.concatenate(intra, axis=1) + y_inter)
                locs.append(_dot_tn(bm[:, ns], xdec[:, gs]))
            st_ref[...] = st * jnp.exp(cs_end) + jnp.concatenate(locs, axis=1)
            y = jnp.concatenate(ys, axis=1)
            seq_rows = pl.ds(pl.multiple_of(tile * ts + ci * CHUNK, CHUNK), CHUNK)
            if rev:
                y = y + yf_ref[seq_rows, :] + x * dsk_ref[...]
                y = y * _silu(u_ref[rows, 0:SSD_W])
                parts = []
                for g in range(SSD_GROUPS):
                    gs = slice(g * SSD_GW, (g + 1) * SSD_GW)
                    parts.append(_rmsnorm(y[:, gs], ng_ref[:, gs]))
                o_ref[rows, :] = jnp.concatenate(parts, axis=1)
            else:
                yf_ref[seq_rows, :] = y
            return carry

        lax.fori_loop(0, nsub, sub, 0, unroll=True)

    pl.when(d == 0)(functools.partial(run, False))
    pl.when(d == 1)(functools.partial(run, True))


def _ssd(u_c, conv_w, conv_b, dt_bias, a_log, d_skip, norm_g, ts):
    bn, s, width = u_c.shape
    nc = s // ts
    main, prev, nxt = _seq_specs(ts, nc, width)
    consts = [conv_w, conv_b, dt_bias, a_log, d_skip, norm_g]
    return pl.pallas_call(
        functools.partial(_ssd_kernel, ts=ts, nc=nc),
        grid=(bn, 2, nc),
        in_specs=[main, prev, nxt] + [_const_spec(a.shape) for a in consts],
        out_specs=_out_spec(ts, nc, SSD_W),
        out_shape=jax.ShapeDtypeStruct((bn, s, SSD_W), F32),
        scratch_shapes=[pltpu.VMEM((s, SSD_W), F32), pltpu.VMEM((SSD_N, SSD_W), F32),
                        pltpu.VMEM((ts, SSD_CONV), F32)],
        compiler_params=_cparams(3),
        name="ssd",
    )(u_c, u_c, u_c, *consts)


HY_W = 512


def _hyena_pre_kernel(u_ref, prev_ref, next_ref, cw_ref, cb_ref, x0_ref, p_ref, *, nc):
    c = pl.program_id(1)
    uc = _conv_rows(prev_ref[...], u_ref[...], next_ref[...], cw_ref, cb_ref[...], 1, c == 0, c == nc - 1)
    x0_ref[...] = uc[:, 0:HY_W]
    p_ref[...] = uc[:, 2 * HY_W:3 * HY_W] * uc[:, HY_W:2 * HY_W]


def _hyena_pre(u_d, conv_w, conv_b, ts):
    bn, s, width = u_d.shape
    nc = s // ts
    per = ts // HALO
    nb = nc * per
    main = pl.BlockSpec((None, ts, width), lambda b, c: (b, c, 0))
    prev = pl.BlockSpec((None, HALO, width), lambda b, c: (b, jnp.maximum(c * per - 1, 0), 0))
    nxt = pl.BlockSpec((None, HALO, width), lambda b, c: (b, jnp.minimum((c + 1) * per, nb - 1), 0))
    out = pl.BlockSpec((None, ts, HY_W), lambda b, c: (b, c, 0))
    return pl.pallas_call(
        functools.partial(_hyena_pre_kernel, nc=nc),
        grid=(bn, nc),
        in_specs=[main, prev, nxt, pl.BlockSpec(conv_w.shape, lambda b, c: (0, 0)),
                  pl.BlockSpec(conv_b.shape, lambda b, c: (0, 0))],
        out_specs=[out, out],
        out_shape=[jax.ShapeDtypeStruct((bn, s, HY_W), F32)] * 2,
        compiler_params=_cparams(2),
        name="hyena_pre",
    )(u_d, u_d, u_d, conv_w, conv_b)


def _hyena_positions(length):
    t = np.linspace(0.0, 1.0, length)[:, None]
    w = (2.0 * math.pi / length) * np.arange(length)[:, None]
    f = np.linspace(1e-4, HY_BANDS - 1, HY_BANDS)[None, :]
    z = np.concatenate([t, np.cos(f * w), -np.sin(f * w)], axis=-1)
    return jnp.asarray(np.pad(z, ((0, 0), (0, LANES - HY_EMB))), F32)


def _hyena_filter_kernel(z_ref, w1_ref, b1_ref, w2_ref, b2_ref, w3_ref, b3_ref, w4_ref, fr_ref, dc_ref, o_ref):
    z = z_ref[...]
    h = jnp.sin(fr_ref[0:1, :] * (_dot(z, w1_ref[...]) + b1_ref[...]))
    h = jnp.sin(fr_ref[1:2, :] * (_dot(h, w2_ref[...]) + b2_ref[...]))
    h = jnp.sin(fr_ref[2:3, :] * (_dot(h, w3_ref[...]) + b3_ref[...]))
    o_ref[...] = _dot(h, w4_ref[...]) * jnp.exp(-z[:, 0:1] * jnp.abs(dc_ref[...]))


def _hyena_filters(z, w1, b1, w2, b2, w3, b3, w4, freq, decay, tl=512):
    length = z.shape[0]
    tl = min(tl, length)
    consts = [jnp.pad(w1, ((0, LANES - HY_EMB), (0, 0))), b1.reshape(1, -1), w2, b2.reshape(1, -1), w3,
              b3.reshape(1, -1), w4, freq, decay.reshape(1, -1)]
    return pl.pallas_call(
        _hyena_filter_kernel,
        grid=(length // tl,),
        in_specs=[pl.BlockSpec((tl, LANES), lambda i: (i, 0))]
        + [pl.BlockSpec(a.shape, lambda i: (0, 0)) for a in consts],
        out_specs=pl.BlockSpec((tl, 2 * HY_W), lambda i: (i, 0)),
        out_shape=jax.ShapeDtypeStruct((length, 2 * HY_W), F32),
        compiler_params=_cparams(1),
        name="hyena_filter",
    )(z, *consts)


FFT_N2 = 128


def _fft_tables(length):
    n = 2 * length
    n2 = FFT_N2
    n1 = n // n2
    k = np.arange(n1)
    idx = (n2 * np.outer(k, np.arange(n1))[None] + np.arange(n2)[:, None, None] * k[None, :, None]) % n
    ang = -2.0 * math.pi * idx / n
    stage1 = np.concatenate([np.cos(ang), np.sin(ang)], axis=1)
    ang2 = -2.0 * math.pi * (np.outer(np.arange(n2), np.arange(n2)) % n2) / n2
    fr, fi = np.cos(ang2), np.sin(ang2)
    stage2 = np.block([[fr, -fi], [fi, fr]])
    stage2_inv = np.block([[fr, fi], [-fi, fr]])
    angt = np.transpose(ang, (0, 2, 1))[:, :n1 // 2, :]
    stage3 = np.concatenate([np.cos(angt), np.sin(angt)], axis=2) / n
    as_bf16 = lambda a: jnp.asarray(a, F32).astype(BF16)
    return dict(stage1=as_bf16(stage1), stage2=as_bf16(stage2), stage2_inv=as_bf16(stage2_inv),
                stage3=as_bf16(stage3))


def _fft_stage1_kernel(x_ref, f_ref, o_ref):
    n1 = o_ref.shape[1]
    for j in range(x_ref.shape[1]):
        a = jnp.dot(f_ref[j], x_ref[:, j, :].astype(BF16), preferred_element_type=F32)
        o_ref[0, :, j, :] = a[:n1]
        o_ref[1, :, j, :] = a[n1:]


def _fft_stage1(x4, table, tn2=8):
    bn, kk, n2, ch = x4.shape
    n1 = table.shape[1] // 2
    return pl.pallas_call(
        _fft_stage1_kernel,
        grid=(n2 // tn2, bn),
        in_specs=[pl.BlockSpec((None, kk, tn2, ch), lambda i, b: (b, 0, i, 0)),
                  pl.BlockSpec((tn2, 2 * n1, kk), lambda i, b: (i, 0, 0))],
        out_specs=pl.BlockSpec((None, 2, n1, tn2, ch), lambda i, b: (b, 0, 0, i, 0)),
        out_shape=jax.ShapeDtypeStruct((bn, 2, n1, n2, ch), F32),
        compiler_params=_cparams(2),
        name="fft_stage1",
    )(x4, table)


def _fft_stage2_kernel(a_ref, e_ref, *rest, with_filter):
    if with_filter:
        h_ref, ei_ref, o_ref = rest
    else:
        (o_ref,) = rest
    n2 = a_ref.shape[2]
    for j in range(a_ref.shape[1]):
        x = jnp.dot(e_ref[...], jnp.concatenate([a_ref[0, j], a_ref[1, j]], axis=0).astype(BF16),
                    preferred_element_type=F32)
        if with_filter:
            xr, xi = x[:n2], x[n2:]
            hr, hi = h_ref[0, j], h_ref[1, j]
            z = jnp.concatenate([xr * hr - xi * hi, xr * hi + xi * hr], axis=0)
            x = jnp.dot(ei_ref[...], z.astype(BF16), preferred_element_type=F32)
        o_ref[0, j] = x[:n2]
        o_ref[1, j] = x[n2:]


def _fft_stage2(a, tables, spectrum=None, tk1=4):
    bn, _, n1, n2, ch = a.shape
    tk1 = min(tk1, n1)
    blk = lambda bmap: pl.BlockSpec((None, 2, tk1, n2, ch), bmap)
    mat = pl.BlockSpec((2 * n2, 2 * n2), lambda i, b: (0, 0))
    data_map = lambda i, b: (b, 0, i, 0, 0)
    if spectrum is None:
        ins, specs = [a, tables["stage2"]], [blk(data_map), mat]
    else:
        ins = [a, tables["stage2"], spectrum, tables["stage2_inv"]]
        specs = [blk(data_map), mat, blk(lambda i, b: (0, 0, i, 0, 0)), mat]
    return pl.pallas_call(
        functools.partial(_fft_stage2_kernel, with_filter=spectrum is not None),
        grid=(n1 // tk1, bn),
        in_specs=specs,
        out_specs=blk(data_map),
        out_shape=jax.ShapeDtypeStruct(a.shape, F32),
        compiler_params=_cparams(2),
        name="fft_stage2",
    )(*ins)


def _fft_stage3_kernel(b_ref, g_ref, p_ref, x0_ref, bias_ref, o_ref):
    for j in range(p_ref.shape[1]):
        rhs = jnp.concatenate([b_ref[0, :, j, :], b_ref[1, :, j, :]], axis=0).astype(BF16)
        y = jnp.dot(g_ref[j], rhs, preferred_element_type=F32)
        o_ref[:, j, :] = x0_ref[:, j, :] * (y + p_ref[:, j, :] * bias_ref[...])


def _fft_stage3(bm, table, p4, x04, bias, tn2=8):
    bn, kk, n2, ch = p4.shape
    n1 = bm.shape[2]
    seq = pl.BlockSpec((None, kk, tn2, ch), lambda i, b: (b, 0, i, 0))
    return pl.pallas_call(
        _fft_stage3_kernel,
        grid=(n2 // tn2, bn),
        in_specs=[pl.BlockSpec((None, 2, n1, tn2, ch), lambda i, b: (b, 0, 0, i, 0)),
                  pl.BlockSpec((tn2, kk, 2 * n1), lambda i, b: (i, 0, 0)),
                  seq, seq, pl.BlockSpec((1, ch), lambda i, b: (0, 0))],
        out_specs=seq,
        out_shape=jax.ShapeDtypeStruct(p4.shape, F32),
        compiler_params=_cparams(2),
        name="fft_stage3",
    )(bm, table, p4, x04, bias.reshape(1, ch))


def _long_conv_gated(p, x0, h_fwd, h_bwd, bias, tables):
    bn, length, ch = p.shape
    n2 = FFT_N2
    n1 = 2 * length // n2
    filt = jnp.concatenate([h_fwd, jnp.zeros((1, ch), F32), jnp.flip(h_bwd[1:], axis=0)], axis=0)
    spectrum = _fft_stage2(_fft_stage1(filt.reshape(1, n1, n2, ch), tables["stage1"]), tables)
    p4 = p.reshape(bn, n1 // 2, n2, ch)
    a = _fft_stage1(p4, tables["stage1"][:, :, :n1 // 2])
    bm = _fft_stage2(a, tables, spectrum)
    y = _fft_stage3(bm, tables["stage3"], p4, x0.reshape(bn, n1 // 2, n2, ch), bias)
    return y.reshape(bn, length, ch)


def _merge_kernel(x_ref, g_ref, ya_ref, yb_ref, yc_ref, yd_ref, wg_ref, wb_ref, wo_ref, o_ref):
    x = x_ref[...]
    d = x.shape[1]
    xn = _rmsnorm(x, g_ref[...]).astype(BF16)
    merged = None
    for n, y_ref in enumerate((ya_ref, yb_ref, yc_ref, yd_ref)):
        gate = jax.nn.sigmoid(jnp.dot(xn, wg_ref[:, n * d:(n + 1) * d], preferred_element_type=F32))
        term = gate * _dot(y_ref[...], wb_ref[n])
        merged = term if merged is None else merged + term
    o_ref[...] = x + _dot(merged, wo_ref[...])


def _merge(x2, g, ys, w_gate, w_branch, w_out, tm=256):
    t, d = x2.shape
    tm = min(tm, t)
    row = lambda w: pl.BlockSpec((tm, w), lambda i: (i, 0))
    full = lambda a: pl.BlockSpec(a.shape, lambda i: (0,) * a.ndim)
    return pl.pallas_call(
        _merge_kernel,
        grid=(t // tm,),
        in_specs=[row(d), pl.BlockSpec((1, d), lambda i: (0, 0))] + [row(y.shape[1]) for y in ys]
        + [full(w_gate), full(w_branch), full(w_out)],
        out_specs=row(d),
        out_shape=jax.ShapeDtypeStruct((t, d), F32),
        compiler_params=_cparams(1),
        name="merge",
    )(x2, g.reshape(1, d), *ys, w_gate, w_branch, w_out)


def _xattn_kernel(x_ref, g_ref, kv_ref, wq_ref, wo_ref, o_ref):
    x = x_ref[...]
    d = x.shape[1]
    hd = d // XA_HEADS
    q = _dot(_rmsnorm(x, g_ref[...]), wq_ref[...])
    outs = []
    for h in range(XA_HEADS):
        k = kv_ref[:, h * hd:(h + 1) * hd]
        v = kv_ref[:, d + h * hd:d + (h + 1) * hd]
        s = _dot_nt(q[:, h * hd:(h + 1) * hd], k) * (hd ** -0.5)
        s = s - jnp.max(s, axis=-1, keepdims=True)
        e = jnp.exp(s)
        p = e / jnp.sum(e, axis=-1, keepdims=True)
        outs.append(_dot(p, v))
    o_ref[...] = x + _dot(jnp.concatenate(outs, axis=1), wo_ref[...])


def _xattn(x3, g, kv, wq, wo, tq=512):
    bn, s, d = x3.shape
    m = kv.shape[1]
    tq = min(tq, s)
    return pl.pallas_call(
        _xattn_kernel,
        grid=(bn, s // tq),
        in_specs=[pl.BlockSpec((None, tq, d), lambda b, i: (b, i, 0)),
                  pl.BlockSpec((1, d), lambda b, i: (0, 0)),
                  pl.BlockSpec((None, m, 2 * d), lambda b, i: (b, 0, 0)),
                  pl.BlockSpec(wq.shape, lambda b, i: (0, 0)),
                  pl.BlockSpec(wo.shape, lambda b, i: (0, 0))],
        out_specs=pl.BlockSpec((None, tq, d), lambda b, i: (b, i, 0)),
        out_shape=jax.ShapeDtypeStruct((bn, s, d), F32),
        compiler_params=_cparams(2),
        name="xattn",
    )(x3, g.reshape(1, d), kv, wq, wo)


def _router_kernel(x_ref, g_ref, wr_ref, aff_ref, xt_ref):
    xn = _rmsnorm(x_ref[...], g_ref[...])
    xt_ref[...] = xn.astype(xt_ref.dtype)
    logits = jnp.dot(xn.astype(BF16), wr_ref[...], preferred_element_type=F32)
    lane = lax.broadcasted_iota(jnp.int32, logits.shape, 1)
    logits = jnp.where(lane < N_EXPERTS, logits, -jnp.inf)
    e = jnp.exp(logits - jnp.max(logits, axis=-1, keepdims=True))
    aff_ref[...] = e / jnp.sum(e, axis=-1, keepdims=True)


def _router(x2, g, wr_pad, tm=512):
    t, d = x2.shape
    tm = min(tm, t)
    return pl.pallas_call(
        _router_kernel,
        grid=(t // tm,),
        in_specs=[pl.BlockSpec((tm, d), lambda i: (i, 0)), pl.BlockSpec((1, d), lambda i: (0, 0)),
                  pl.BlockSpec(wr_pad.shape, lambda i: (0, 0))],
        out_specs=[pl.BlockSpec((tm, LANES), lambda i: (i, 0)), pl.BlockSpec((tm, d), lambda i: (i, 0))],
        out_shape=[jax.ShapeDtypeStruct((t, LANES), F32), jax.ShapeDtypeStruct((t, d), BF16)],
        compiler_params=_cparams(1),
        name="router",
    )(x2, g.reshape(1, d), wr_pad)


def _expert_kernel(xe_ref, gate_ref, wg_ref, wu_ref, wd_ref, o_ref):
    xe = xe_ref[...]
    hdn = _silu(jnp.dot(xe, wg_ref[...], preferred_element_type=F32)) * jnp.dot(
        xe, wu_ref[...], preferred_element_type=F32)
    o_ref[...] = _dot(hdn, wd_ref[...]) * gate_ref[...]


def _experts(xe, gate, w_gate, w_up, w_down, tm=512):
    ne, cap, d = xe.shape
    ff = w_gate.shape[2]
    tm = min(tm, cap)
    wspec = lambda a: pl.BlockSpec((None,) + a.shape[1:], lambda e, i: (e, 0, 0))
    return pl.pallas_call(
        _expert_kernel,
        grid=(ne, cap // tm),
        in_specs=[pl.BlockSpec((None, tm, d), lambda e, i: (e, i, 0)),
                  pl.BlockSpec((None, tm, 1), lambda e, i: (e, i, 0)),
                  wspec(w_gate), wspec(w_up), wspec(w_down)],
        out_specs=pl.BlockSpec((None, tm, d), lambda e, i: (e, i, 0)),
        out_shape=jax.ShapeDtypeStruct((ne, cap, d), F32),
        compiler_params=_cparams(2),
        name="experts",
    )(xe, gate[..., None], w_gate, w_up, w_down)


def _final_norm_kernel(x_ref, g_ref, o_ref):
    o_ref[...] = _rmsnorm(x_ref[...], g_ref[...])


def _final_norm(x2, g, tm=512):
    t, d = x2.shape
    tm = min(tm, t)
    return pl.pallas_call(
        _final_norm_kernel,
        grid=(t // tm,),
        in_specs=[pl.BlockSpec((tm, d), lambda i: (i, 0)), pl.BlockSpec((1, d), lambda i: (0, 0))],
        out_specs=pl.BlockSpec((tm, d), lambda i: (i, 0)),
        out_shape=jax.ShapeDtypeStruct((t, d), F32),
        compiler_params=_cparams(1),
        name="final_norm",
    )(x2, g.reshape(1, d))


def _pad_cols(w, n):
    return jnp.pad(w, ((0, 0), (0, n - w.shape[1])))


def _block_diag(w):
    h, blk, _ = w.shape
    eye = jnp.eye(h, dtype=w.dtype)
    return jnp.einsum("hij,hk->hikj", w, eye).reshape(h * blk, h * blk)


def _row128(v):
    return _pad_cols(v.reshape(1, -1), LANES)


def _prep_layer(p, l):
    d = p["w_in"].shape[1]
    w_in = p["w_in"][l]
    sizes = (512, 512, 256, 256, 512, 512, 32, 512, 768, 16, 1536, 4 * d)
    offs = np.concatenate([[0], np.cumsum(sizes)])
    seg = [w_in[:, offs[i]:offs[i + 1]] for i in range(len(sizes))]
    a_x, a_y, b_q, b_k, b_v, b_g, b_lr, c_z, c_xbc, c_dt, d_u, gates = seg
    w_a = jnp.concatenate([a_x, a_y], axis=1)
    w_b = jnp.concatenate([b_q, b_k, b_v, b_g, _pad_cols(b_lr[:, :GLA_RANK], LANES),
                           _pad_cols(b_lr[:, GLA_RANK:], LANES)], axis=1)
    w_c = jnp.concatenate([c_z, c_xbc, _pad_cols(c_dt, LANES)], axis=1)
    out = {
        "w_mix": [w.astype(BF16) for w in (w_a, w_b, w_c, d_u)],
        "w_gates": gates.astype(BF16),
        "lru_gate_w": jnp.stack([jnp.concatenate([_block_diag(p["lru_gate_w"][l, dd, 0]),
                                                  _block_diag(p["lru_gate_w"][l, dd, 1])], axis=1)
                                 for dd in range(2)]).astype(BF16),
        "lru_gate_b": p["lru_gate_b"][l].reshape(2, 1, 2 * LRU_W),
        "lru_lambda": p["lru_lambda"][l].reshape(2, 1, LRU_W),
        "lru_conv_w": p["lru_conv_w"][l],
        "lru_conv_b": p["lru_conv_b"][l].reshape(1, -1),
        "gla_wa": jnp.pad(p["gla_wa2"][l], ((0, 0), (0, LANES - GLA_RANK), (0, 0))).astype(BF16),
        "gla_ba": p["gla_ba"][l].reshape(2, 1, GLA_QK),
        "gla_norm": p["gla_norm"][l].reshape(1, GLA_DV),
        "ssd_conv_w": p["ssd_conv_w"][l],
        "ssd_conv_b": p["ssd_conv_b"][l].reshape(1, -1),
        "ssd_dt_bias": _row128(p["ssd_dt_bias"][l]),
        "ssd_a_log": _row128(p["ssd_a_log"][l]),
        "ssd_d": jnp.repeat(p["ssd_d"][l], SSD_P).reshape(1, SSD_W),
        "ssd_norm": p["ssd_norm"][l].reshape(1, SSD_W),
        "hy_conv_w": p["hy_conv_w"][l],
        "hy_conv_b": p["hy_conv_b"][l].reshape(1, -1),
        "w_branch": p["w_branch"][l].astype(BF16),
        "w_out": p["w_out"][l].astype(BF16),
        "xa_wq": p["xa_wq"][l].astype(BF16),
        "xa_wkv": p["xa_wkv"][l].astype(BF16),
        "xa_wo": p["xa_wo"][l].astype(BF16),
        "router_w": _pad_cols(p["router_w"][l], LANES).astype(BF16),
        "exp_w_gate": p["exp_w_gate"][l].astype(BF16),
        "exp_w_up": p["exp_w_up"][l].astype(BF16),
        "exp_w_down": p["exp_w_down"][l].astype(BF16),
    }
    out.update({k: p[k][l] for k in RAW_KEYS})
    return out


RAW_KEYS = ("norm_mix", "norm_xa", "norm_mem", "norm_ffn", "hy_w1", "hy_b1", "hy_w2", "hy_b2", "hy_w3", "hy_b3",
            "hy_w4", "hy_freq", "hy_decay", "hy_bias")


def _seq_tile(s):
    return min(256, s)


def _seq_consts(s):
    return dict(z=_hyena_positions(s), fft=_fft_tables(s))


def _hyena(u_d, q, consts, ts):
    x0, pg = _hyena_pre(u_d, q["hy_conv_w"], q["hy_conv_b"], ts)
    h = _hyena_filters(consts["z"], q["hy_w1"], q["hy_b1"], q["hy_w2"], q["hy_b2"], q["hy_w3"], q["hy_b3"],
                       q["hy_w4"], q["hy_freq"], q["hy_decay"])
    return _long_conv_gated(pg, x0, h[:, :HY_W], h[:, HY_W:], q["hy_bias"], consts["fft"])


def _xattn_layer(x3, mem, q):
    bn, m, d = mem.shape
    (kv,) = _norm_proj(mem.reshape(bn * m, d), q["norm_mem"], [q["xa_wkv"]], out_dtype=BF16)
    return _xattn(x3, q["norm_xa"], kv.reshape(bn, m, 2 * d), q["xa_wq"], q["xa_wo"])


def _moe_layer(x2, q):
    t, d = x2.shape
    cap = max(1, EC_CAPACITY * t // N_EXPERTS)
    aff, xt = _router(x2, q["norm_ffn"], q["router_w"])
    gate, idx = lax.top_k(aff[:, :N_EXPERTS].T, cap)
    ye = _experts(xt[idx], gate, q["exp_w_gate"], q["exp_w_up"], q["exp_w_down"])
    return x2.at[idx.reshape(-1)].add(ye.reshape(-1, d))


def _layer(x, mem, q, consts):
    bn, s, d = x.shape
    t = bn * s
    ts = _seq_tile(s)
    x2 = x.reshape(t, d)

    u_a, u_b, u_c, u_d = _norm_proj(x2, q["norm_mix"], q["w_mix"])
    y_a = _lru(u_a.reshape(bn, s, -1), q["lru_conv_w"], q["lru_conv_b"], q["lru_gate_w"], q["lru_gate_b"],
               q["lru_lambda"], ts)
    y_b = _gla(u_b.reshape(bn, s, -1), q["gla_wa"], q["gla_ba"], q["gla_norm"], ts)
    y_c = _ssd(u_c.reshape(bn, s, -1), q["ssd_conv_w"], q["ssd_conv_b"], q["ssd_dt_bias"], q["ssd_a_log"],
               q["ssd_d"], q["ssd_norm"], ts)
    y_d = _hyena(u_d.reshape(bn, s, -1), q, consts, ts)
    ys = [y.reshape(t, -1) for y in (y_a, y_b, y_c, y_d)]
    x2 = _merge(x2, q["norm_mix"], ys, q["w_gates"], q["w_branch"], q["w_out"])

    x3 = _xattn_layer(x2.reshape(bn, s, d), mem, q)

    return _moe_layer(x3.reshape(t, d), q).reshape(bn, s, d)


def kernel(x_prompt, x_sample, mem_prompt, mem_sample, norm_mix, w_in, lru_conv_w, lru_conv_b, lru_gate_w, lru_gate_b, lru_lambda, gla_wa2, gla_ba, gla_norm, ssd_conv_w, ssd_conv_b, ssd_dt_bias, ssd_a_log, ssd_d, ssd_norm, hy_conv_w, hy_conv_b, hy_w1, hy_b1, hy_w2, hy_b2, hy_w3, hy_b3, hy_w4, hy_freq, hy_decay, hy_bias, w_branch, w_out, norm_xa, norm_mem, xa_wq, xa_wkv, xa_wo, norm_ffn, router_w, exp_w_gate, exp_w_up, exp_w_down, final_norm):
    p = dict(norm_mix=norm_mix, w_in=w_in, lru_conv_w=lru_conv_w, lru_conv_b=lru_conv_b, lru_gate_w=lru_gate_w,
             lru_gate_b=lru_gate_b, lru_lambda=lru_lambda, gla_wa2=gla_wa2, gla_ba=gla_ba, gla_norm=gla_norm,
             ssd_conv_w=ssd_conv_w, ssd_conv_b=ssd_conv_b, ssd_dt_bias=ssd_dt_bias, ssd_a_log=ssd_a_log,
             ssd_d=ssd_d, ssd_norm=ssd_norm, hy_conv_w=hy_conv_w, hy_conv_b=hy_conv_b, hy_w1=hy_w1, hy_b1=hy_b1,
             hy_w2=hy_w2, hy_b2=hy_b2, hy_w3=hy_w3, hy_b3=hy_b3, hy_w4=hy_w4, hy_freq=hy_freq, hy_decay=hy_decay,
             hy_bias=hy_bias, w_branch=w_branch, w_out=w_out, norm_xa=norm_xa, norm_mem=norm_mem, xa_wq=xa_wq,
             xa_wkv=xa_wkv, xa_wo=xa_wo, norm_ffn=norm_ffn, router_w=router_w, exp_w_gate=exp_w_gate,
             exp_w_up=exp_w_up, exp_w_down=exp_w_down)
    depth = w_in.shape[0]
    stacked = jax.tree.map(lambda *a: jnp.stack(a), *[_prep_layer(p, l) for l in range(depth)])
    outs = []
    for x, mem in ((x_prompt, mem_prompt), (x_sample, mem_sample)):
        consts = _seq_consts(x.shape[1])
        x, _ = lax.scan(lambda xc, q, mem=mem, consts=consts: (_layer(xc, mem, q, consts), None), x, stacked)
        bn, s, d = x.shape
        outs.append(_final_norm(x.reshape(bn * s, d), final_norm).reshape(bn, s, d))
    return tuple(outs)
```

```python
import functools
import math

import jax
import jax.numpy as jnp
import numpy as np
from jax import lax
from jax.experimental import pallas as pl
from jax.experimental.pallas import tpu as pltpu

F32 = jnp.float32
BF16 = jnp.bfloat16

EPS = 1e-6
CHUNK = 64
HALO = 8
LANES = 128
VMEM_LIMIT_BYTES = 56 * 1024 * 1024

LRU_C = 8.0
GLA_HEADS, GLA_DK, GLA_DV, GLA_RANK, GLA_TAU = 4, 64, 128, 16, 16.0
SSD_HEADS, SSD_P, SSD_GROUPS, SSD_N = 8, 64, 2, 64
XA_HEADS = 4
N_EXPERTS, EC_CAPACITY = 16, 2
HY_EMB = 33
HY_BANDS = (HY_EMB - 1) // 2


def _cparams(ndims):
    return pltpu.CompilerParams(dimension_semantics=("arbitrary",) * ndims,
                                vmem_limit_bytes=VMEM_LIMIT_BYTES)


def _rmsnorm(x, g):
    xf = x.astype(F32)
    return xf * lax.rsqrt(jnp.mean(xf * xf, axis=-1, keepdims=True) + EPS) * g


def _softplus(x):
    return jnp.maximum(x, 0.0) + jnp.log1p(jnp.exp(-jnp.abs(x)))


def _silu(x):
    return x * jax.nn.sigmoid(x)


def _gelu_tanh(x):
    return 0.5 * x * (1.0 + jnp.tanh(math.sqrt(2.0 / math.pi) * (x + 0.044715 * (x * x * x))))


def _dot(a, b):
    return jnp.dot(a.astype(BF16), b.astype(BF16), preferred_element_type=F32)


def _dot_nt(a, b):
    return lax.dot_general(a.astype(BF16), b.astype(BF16), (((1,), (1,)), ((), ())),
                           preferred_element_type=F32)


def _dot_tn(a, b):
    return lax.dot_general(a.astype(BF16), b.astype(BF16), (((0,), (0,)), ((), ())),
                           preferred_element_type=F32)


def _split_hi_lo(x):
    hi = x.astype(BF16)
    lo = (x - hi.astype(F32)).astype(BF16)
    return hi, lo


def _dot_exact_rhs(sel, x):
    hi, lo = _split_hi_lo(x)
    sel = sel.astype(BF16)
    return (jnp.dot(sel, hi, preferred_element_type=F32) + jnp.dot(sel, lo, preferred_element_type=F32))


def _dot_exact_lhs(x, sel):
    hi, lo = _split_hi_lo(x)
    sel = sel.astype(BF16)
    return (jnp.dot(hi, sel, preferred_element_type=F32) + jnp.dot(lo, sel, preferred_element_type=F32))


def _transpose_exact(x):
    n = x.shape[1]
    eye = (lax.broadcasted_iota(jnp.int32, (n, n), 0) == lax.broadcasted_iota(jnp.int32, (n, n), 1)).astype(BF16)
    hi, lo = _split_hi_lo(x)
    dn = (((1,), (1,)), ((), ()))
    return (lax.dot_general(eye, hi, dn, preferred_element_type=F32)
            + lax.dot_general(eye, lo, dn, preferred_element_type=F32))


def _tri(n, rev):
    l = lax.broadcasted_iota(jnp.int32, (n, n), 0)
    s = lax.broadcasted_iota(jnp.int32, (n, n), 1)
    return (s >= l) if rev else (s <= l)


def _conv_rows(prev, main, nxt, w_ref, b, left, first, last):
    ts = main.shape[0]
    prev = jnp.where(first, 0.0, prev)
    nxt = jnp.where(last, 0.0, nxt)
    ext = jnp.concatenate([prev, main, nxt], axis=0)
    n = ext.shape[0]
    out = b
    for j in range(w_ref.shape[0]):
        sh = (left - j) % n
        rolled = ext if sh == 0 else pltpu.roll(ext, sh, 0)
        out = out + w_ref[j:j + 1, :] * rolled[HALO:HALO + ts]
    return out


def _scan_rows(a, b, rev):
    n = a.shape[0]
    row = lax.broadcasted_iota(jnp.int32, a.shape, 0)
    s = 1
    while s < n:
        if rev:
            a_sh, b_sh, ok = pltpu.roll(a, n - s, 0), pltpu.roll(b, n - s, 0), row < n - s
        else:
            a_sh, b_sh, ok = pltpu.roll(a, s, 0), pltpu.roll(b, s, 0), row >= s
        b = b + a * jnp.where(ok, b_sh, 0.0)
        a = a * jnp.where(ok, a_sh, 1.0)
        s *= 2
    return a, b


def _inproj_kernel(x_ref, g_ref, *refs):
    n = len(refs) // 2
    xn = _rmsnorm(x_ref[...], g_ref[...]).astype(BF16)
    for w_ref, o_ref in zip(refs[:n], refs[n:]):
        o_ref[...] = jnp.dot(xn, w_ref[...], preferred_element_type=F32).astype(o_ref.dtype)


def _norm_proj(x2, g, ws, out_dtype=F32, tm=256):
    t, d = x2.shape
    tm = min(tm, t)
    in_specs = [pl.BlockSpec((tm, d), lambda i: (i, 0)), pl.BlockSpec((1, d), lambda i: (0, 0))]
    in_specs += [pl.BlockSpec(w.shape, lambda i: (0, 0)) for w in ws]
    outs = pl.pallas_call(
        _inproj_kernel,
        grid=(t // tm,),
        in_specs=in_specs,
        out_specs=[pl.BlockSpec((tm, w.shape[1]), lambda i: (i, 0)) for w in ws],
        out_shape=[jax.ShapeDtypeStruct((t, w.shape[1]), out_dtype) for w in ws],
        compiler_params=_cparams(1),
        name="norm_proj",
    )(x2, g.reshape(1, d), *ws)
    return outs


def _seq_specs(ts, nc, width):
    per = ts // HALO
    nb = nc * per

    def tile(b, d, c):
        return c + d * (nc - 1 - 2 * c)

    main = pl.BlockSpec((None, ts, width), lambda b, d, c: (b, tile(b, d, c), 0))
    prev = pl.BlockSpec((None, HALO, width), lambda b, d, c: (b, jnp.maximum(tile(b, d, c) * per - 1, 0), 0))
    nxt = pl.BlockSpec((None, HALO, width), lambda b, d, c: (b, jnp.minimum((tile(b, d, c) + 1) * per, nb - 1), 0))
    return main, prev, nxt


def _out_spec(ts, nc, width):
    return pl.BlockSpec((None, ts, width), lambda b, d, c: (b, nc - 1 - d * c, 0))


def _dir_spec(shape):
    return pl.BlockSpec((None,) + tuple(shape), lambda b, d, c: (d,) + (0,) * len(shape))


def _const_spec(shape):
    return pl.BlockSpec(tuple(shape), lambda b, d, c: (0,) * len(shape))


LRU_W = 512


def _lru_kernel(u_ref, prev_ref, next_ref, cw_ref, cb_ref, gw_ref, gb_ref, lam_ref, o_ref, hf_ref, carry_ref,
                *, ts, nc):
    d = pl.program_id(1)
    c = pl.program_id(2)

    @pl.when(c == 0)
    def _():
        carry_ref[...] = jnp.zeros_like(carry_ref)

    def run(rev):
        tile = (nc - 1 - c) if rev else c
        xc = _conv_rows(prev_ref[:, :LRU_W], u_ref[:, :LRU_W], next_ref[:, :LRU_W], cw_ref, cb_ref[...], 2,
                        tile == 0, tile == nc - 1)
        gl = _dot(xc, gw_ref[...]) + gb_ref[...]
        r = jax.nn.sigmoid(gl[:, :LRU_W])
        i = jax.nn.sigmoid(gl[:, LRU_W:])
        log_a = -LRU_C * r * _softplus(-lam_ref[...])
        a = jnp.exp(log_a)
        b = jnp.sqrt(1.0 - jnp.exp(2.0 * log_a)) * (i * xc)
        a_cum, h = _scan_rows(a, b, rev)
        h = h + a_cum * carry_ref[...]
        carry_ref[...] = h[0:1, :] if rev else h[ts - 1:ts, :]
        rows = pl.ds(pl.multiple_of(tile * ts, ts), ts)
        if rev:
            o_ref[...] = (hf_ref[rows, :] + h) * _gelu_tanh(u_ref[:, LRU_W:])
        else:
            hf_ref[rows, :] = h

    pl.when(d == 0)(functools.partial(run, False))
    pl.when(d == 1)(functools.partial(run, True))


def _lru(u_a, conv_w, conv_b, gate_w, gate_b, lam, ts):
    bn, s, width = u_a.shape
    nc = s // ts
    main, prev, nxt = _seq_specs(ts, nc, width)
    return pl.pallas_call(
        functools.partial(_lru_kernel, ts=ts, nc=nc),
        grid=(bn, 2, nc),
        in_specs=[main, prev, nxt, _const_spec(conv_w.shape), _const_spec(conv_b.shape),
                  _dir_spec(gate_w.shape[1:]), _dir_spec(gate_b.shape[1:]), _dir_spec(lam.shape[1:])],
        out_specs=_out_spec(ts, nc, LRU_W),
        out_shape=jax.ShapeDtypeStruct((bn, s, LRU_W), F32),
        scratch_shapes=[pltpu.VMEM((s, LRU_W), F32), pltpu.VMEM((1, LRU_W), F32)],
        compiler_params=_cparams(3),
        name="rglru",
    )(u_a, u_a, u_a, conv_w, conv_b, gate_w, gate_b, lam)


GLA_QK = GLA_HEADS * GLA_DK
GLA_V = GLA_HEADS * GLA_DV
GLA_COLS = 2 * GLA_QK + 2 * GLA_V + 2 * LANES


def _gla_kernel(u_ref, wa_ref, ba_ref, ng_ref, o_ref, of_ref, st_ref, *, ts, nc):
    d = pl.program_id(1)
    c = pl.program_id(2)
    nsub = ts // CHUNK

    @pl.when(c == 0)
    def _():
        st_ref[...] = jnp.zeros_like(st_ref)

    def run(rev):
        tile = (nc - 1 - c) if rev else c
        tri = _tri(CHUNK, rev)
        ref_row = CHUNK // 2 if rev else CHUNK // 2 - 1
        last_row = 0 if rev else CHUNK - 1
        code0 = 2 * GLA_QK + 2 * GLA_V + (LANES if rev else 0)
        stacked = (GLA_HEADS * CHUNK, GLA_QK)
        head_mask = (lax.shift_right_logical(lax.broadcasted_iota(jnp.int32, stacked, 0), CHUNK.bit_length() - 1)
                     == lax.shift_right_logical(lax.broadcasted_iota(jnp.int32, stacked, 1), GLA_DK.bit_length() - 1))
        tri_heads = jnp.concatenate([tri] * GLA_HEADS, axis=0)

        stack = lambda x: jnp.where(head_mask, jnp.concatenate([x] * GLA_HEADS, axis=0), 0.0)
        order = list(range(nsub - 1, -1, -1)) if rev else list(range(nsub))

        logits = _dot(u_ref[:, code0:code0 + LANES], wa_ref[...]) + ba_ref[...]
        log_a = -_softplus(-logits) * (1.0 / GLA_TAU)
        work = {}
        for ci in order:
            rows = slice(ci * CHUNK, (ci + 1) * CHUNK)
            q = u_ref[rows, 0:GLA_QK] * (GLA_DK ** -0.5)
            k = u_ref[rows, GLA_QK:2 * GLA_QK]
            v = u_ref[rows, 2 * GLA_QK:2 * GLA_QK + GLA_V]
            b = _dot_exact_rhs(tri, log_a[rows])
            b_mid = b[ref_row:ref_row + 1, :]
            b_end = b[last_row:last_row + 1, :]
            sc = jnp.where(tri_heads, _dot_nt(stack(q * jnp.exp(b - b_mid)), k * jnp.exp(b_mid - b)), 0.0)
            o_intra = _dot(sc, v)
            v_heads = jnp.concatenate([v[:, h * GLA_DV:(h + 1) * GLA_DV] for h in range(GLA_HEADS)], axis=0)
            loc = _dot_tn(v_heads, stack(k * jnp.exp(b_end - b)))
            work[ci] = (o_intra, stack(q * jnp.exp(b)).astype(BF16), jnp.exp(b_end), loc)

        st = st_ref[...]
        outs = {}
        for ci in order:
            o_intra, qe, decay, loc = work[ci]
            o_inter = _dot_nt(qe, st)
            outs[ci] = jnp.concatenate(
                [o_intra[h * CHUNK:(h + 1) * CHUNK, h * GLA_DV:(h + 1) * GLA_DV] + o_inter[h * CHUNK:(h + 1) * CHUNK]
                 for h in range(GLA_HEADS)], axis=1)
            st = st * decay + loc
        st_ref[...] = st

        seq_rows = pl.ds(pl.multiple_of(tile * ts, ts), ts)
        o = jnp.concatenate([outs[ci] for ci in range(nsub)], axis=0)
        if rev:
            o = o + of_ref[seq_rows, :]
            g = u_ref[:, 2 * GLA_QK + GLA_V:2 * GLA_QK + 2 * GLA_V]
            parts = []
            for h in range(GLA_HEADS):
                vs = slice(h * GLA_DV, (h + 1) * GLA_DV)
                parts.append(_rmsnorm(o[:, vs], ng_ref[...]) * _silu(g[:, vs]))
            o_ref[...] = jnp.concatenate(parts, axis=1)
        else:
            of_ref[seq_rows, :] = o

    pl.when(d == 0)(functools.partial(run, False))
    pl.when(d == 1)(functools.partial(run, True))


def _gla(u_b, wa, ba, norm_g, ts):
    bn, s, width = u_b.shape
    nc = s // ts
    main, _, _ = _seq_specs(ts, nc, width)
    return pl.pallas_call(
        functools.partial(_gla_kernel, ts=ts, nc=nc),
        grid=(bn, 2, nc),
        in_specs=[main, _dir_spec(wa.shape[1:]), _dir_spec(ba.shape[1:]), _const_spec(norm_g.shape)],
        out_specs=_out_spec(ts, nc, GLA_V),
        out_shape=jax.ShapeDtypeStruct((bn, s, GLA_V), F32),
        scratch_shapes=[pltpu.VMEM((s, GLA_V), F32), pltpu.VMEM((GLA_DV, GLA_QK), F32)],
        compiler_params=_cparams(3),
        name="gla",
    )(u_b, wa, ba, norm_g)


SSD_W = 512
SSD_BC = SSD_GROUPS * SSD_N
SSD_CONV = SSD_W + 2 * SSD_BC
SSD_COLS = SSD_W + SSD_CONV + LANES
SSD_GW = SSD_W // SSD_GROUPS


def _ssd_kernel(u_ref, prev_ref, next_ref, cw_ref, cb_ref, dtb_ref, alog_ref, dsk_ref, ng_ref, o_ref,
                yf_ref, st_ref, *, ts, nc):
    d = pl.program_id(1)
    c = pl.program_id(2)
    nsub = ts // CHUNK
    xbc = slice(SSD_W, SSD_W + SSD_CONV)
    dtc = slice(SSD_W + SSD_CONV, SSD_COLS)

    @pl.when(c == 0)
    def _():
        st_ref[...] = jnp.zeros_like(st_ref)

    def run(rev):
        tile = (nc - 1 - c) if rev else c
        tri = _tri(CHUNK, rev)
        last_row = 0 if rev else CHUNK - 1
        xs = _silu(_conv_rows(prev_ref[:, xbc], u_ref[:, xbc], next_ref[:, xbc], cw_ref, cb_ref[...], 2,
                              tile == 0, tile == nc - 1))
        lane = lax.broadcasted_iota(jnp.int32, (1, LANES), 1)
        a_row = jnp.where(lane < 2 * SSD_HEADS, -jnp.exp(alog_ref[...]), 0.0)
        er = lax.broadcasted_iota(jnp.int32, (LANES, SSD_W), 0)
        ec = lax.broadcasted_iota(jnp.int32, (LANES, SSD_W), 1)
        expand = er == (lax.shift_right_logical(ec, SSD_P.bit_length() - 1) + (SSD_HEADS if rev else 0))
        h0 = SSD_HEADS if rev else 0
        order = list(range(nsub - 1, -1, -1)) if rev else list(range(nsub))
        rl = lax.broadcasted_iota(jnp.int32, (ts, ts), 0)
        rs = lax.broadcasted_iota(jnp.int32, (ts, ts), 1)
        shift = CHUNK.bit_length() - 1
        same_chunk = lax.shift_right_logical(rl, shift) == lax.shift_right_logical(rs, shift)
        tri_tile = same_chunk & ((rs >= rl) if rev else (rs <= rl))

        dt = _softplus(u_ref[:, dtc] + dtb_ref[...])
        cs = _dot_exact_rhs(tri_tile, dt * a_row)
        cs_t = _transpose_exact(cs)
        both = _dot_exact_lhs(jnp.concatenate([dt, cs], axis=0), expand)
        dt_e, cs_e = both[:ts], both[ts:]
        work = {}
        for ci in order:
            rows = slice(ci * CHUNK, (ci + 1) * CHUNK)
            x = xs[rows, 0:SSD_W]
            bm = xs[rows, SSD_W:SSD_W + SSD_BC]
            cm = xs[rows, SSD_W + SSD_BC:SSD_CONV]
            cs_c = cs_e[rows]
            cs_end = cs_c[last_row:last_row + 1, :]
            xdt = x * dt_e[rows]
            xdec = xdt * jnp.exp(cs_end - cs_c)
            intra, locs = [], []
            for g in range(SSD_GROUPS):
                ns = slice(g * SSD_N, (g + 1) * SSD_N)
                gs = slice(g * SSD_GW, (g + 1) * SSD_GW)
                cb = _dot_nt(cm[:, ns], bm[:, ns])
                for r in range(SSD_HEADS // SSD_GROUPS):
                    hh = g * (SSD_HEADS // SSD_GROUPS) + r
                    j = h0 + hh
                    seg = cs[rows, j:j + 1] - cs_t[j:j + 1, rows]
                    lmat = jnp.exp(jnp.where(tri, seg, -jnp.inf))
                    intra.append(_dot(cb * lmat, xdt[:, hh * SSD_P:(hh + 1) * SSD_P]))
                locs.append(_dot_tn(bm[:, ns], xdec[:, gs]))
            work[ci] = (jnp.concatenate(intra, axis=1), cm.astype(BF16), jnp.exp(cs_c), jnp.exp(cs_end),
                        jnp.concatenate(locs, axis=1))

        st = st_ref[...]
        ys = {}
        for ci in order:
            y_intra, cm, grow, decay, loc = work[ci]
            y_inter = jnp.concatenate(
                [_dot(cm[:, g * SSD_N:(g + 1) * SSD_N], st[:, g * SSD_GW:(g + 1) * SSD_GW])
                 for g in range(SSD_GROUPS)], axis=1)
            ys[ci] = y_intra + y_inter * grow
            st = st * decay + loc
        st_ref[...] = st

        y = jnp.concatenate([ys[ci] for ci in range(nsub)], axis=0)
        seq_rows = pl.ds(pl.multiple_of(tile * ts, ts), ts)
        if rev:
            y = y + yf_ref[seq_rows, :] + xs[:, 0:SSD_W] * dsk_ref[...]
            y = y * _silu(u_ref[:, 0:SSD_W])
            parts = []
            for g in range(SSD_GROUPS):
                gs = slice(g * SSD_GW, (g + 1) * SSD_GW)
                parts.append(_rmsnorm(y[:, gs], ng_ref[:, gs]))
            o_ref[...] = jnp.concatenate(parts, axis=1)
        else:
            yf_ref[seq_rows, :] = y

    pl.when(d == 0)(functools.partial(run, False))
    pl.when(d == 1)(functools.partial(run, True))


def _ssd(u_c, conv_w, conv_b, dt_bias, a_log, d_skip, norm_g, ts):
    bn, s, width = u_c.shape
    nc = s // ts
    main, prev, nxt = _seq_specs(ts, nc, width)
    consts = [conv_w, conv_b, dt_bias, a_log, d_skip, norm_g]
    return pl.pallas_call(
        functools.partial(_ssd_kernel, ts=ts, nc=nc),
        grid=(bn, 2, nc),
        in_specs=[main, prev, nxt] + [_const_spec(a.shape) for a in consts],
        out_specs=_out_spec(ts, nc, SSD_W),
        out_shape=jax.ShapeDtypeStruct((bn, s, SSD_W), F32),
        scratch_shapes=[pltpu.VMEM((s, SSD_W), F32), pltpu.VMEM((SSD_N, SSD_W), F32)],
        compiler_params=_cparams(3),
        name="ssd",
    )(u_c, u_c, u_c, *consts)


HY_W = 512


def _hyena_pre_kernel(u_ref, prev_ref, next_ref, cw_ref, cb_ref, x0_ref, p_ref, *, nc):
    c = pl.program_id(1)
    uc = _conv_rows(prev_ref[...], u_ref[...], next_ref[...], cw_ref, cb_ref[...], 1, c == 0, c == nc - 1)
    x0_ref[...] = uc[:, 0:HY_W]
    p_ref[...] = uc[:, 2 * HY_W:3 * HY_W] * uc[:, HY_W:2 * HY_W]


def _hyena_pre(u_d, conv_w, conv_b, ts):
    bn, s, width = u_d.shape
    nc = s // ts
    per = ts // HALO
    nb = nc * per
    main = pl.BlockSpec((None, ts, width), lambda b, c: (b, c, 0))
    prev = pl.BlockSpec((None, HALO, width), lambda b, c: (b, jnp.maximum(c * per - 1, 0), 0))
    nxt = pl.BlockSpec((None, HALO, width), lambda b, c: (b, jnp.minimum((c + 1) * per, nb - 1), 0))
    out = pl.BlockSpec((None, ts, HY_W), lambda b, c: (b, c, 0))
    return pl.pallas_call(
        functools.partial(_hyena_pre_kernel, nc=nc),
        grid=(bn, nc),
        in_specs=[main, prev, nxt, pl.BlockSpec(conv_w.shape, lambda b, c: (0, 0)),
                  pl.BlockSpec(conv_b.shape, lambda b, c: (0, 0))],
        out_specs=[out, out],
        out_shape=[jax.ShapeDtypeStruct((bn, s, HY_W), F32)] * 2,
        compiler_params=_cparams(2),
        name="hyena_pre",
    )(u_d, u_d, u_d, conv_w, conv_b)


def _hyena_positions(length):
    t = np.linspace(0.0, 1.0, length)[:, None]
    w = (2.0 * math.pi / length) * np.arange(length)[:, None]
    f = np.linspace(1e-4, HY_BANDS - 1, HY_BANDS)[None, :]
    z = np.concatenate([t, np.cos(f * w), -np.sin(f * w)], axis=-1)
    z = np.concatenate([z, z[:1], z[:0:-1]], axis=0)
    return jnp.asarray(np.pad(z, ((0, 0), (0, LANES - HY_EMB))), F32)


def _hyena_filter_kernel(z_ref, w1_ref, b1_ref, w2_ref, b2_ref, w3_ref, b3_ref, w4_ref, fr_ref, dc_ref, o_ref):
    i = pl.program_id(0)
    half = pl.num_programs(0) // 2
    z = z_ref[...]
    h = jnp.sin(fr_ref[0:1, :] * (_dot(z, w1_ref[...]) + b1_ref[...]))
    h = jnp.sin(fr_ref[1:2, :] * (_dot(h, w2_ref[...]) + b2_ref[...]))
    h = jnp.sin(fr_ref[2:3, :] * (_dot(h, w3_ref[...]) + b3_ref[...]))
    banks = _dot(h, w4_ref[...]) * jnp.exp(-z[:, 0:1] * jnp.abs(dc_ref[...]))
    out = jnp.where(i < half, banks[:, :HY_W], banks[:, HY_W:])
    row = lax.broadcasted_iota(jnp.int32, out.shape, 0)
    o_ref[...] = jnp.where((i == half) & (row == 0), 0.0, out)


def _hyena_filters(z, w1, b1, w2, b2, w3, b3, w4, freq, decay, tl=512):
    n = z.shape[0]
    tl = min(tl, n // 2)
    consts = [jnp.pad(w1, ((0, LANES - HY_EMB), (0, 0))), b1.reshape(1, -1), w2, b2.reshape(1, -1), w3,
              b3.reshape(1, -1), w4, freq, decay.reshape(1, -1)]
    return pl.pallas_call(
        _hyena_filter_kernel,
        grid=(n // tl,),
        in_specs=[pl.BlockSpec((tl, LANES), lambda i: (i, 0))]
        + [pl.BlockSpec(a.shape, lambda i: (0, 0)) for a in consts],
        out_specs=pl.BlockSpec((tl, HY_W), lambda i: (i, 0)),
        out_shape=jax.ShapeDtypeStruct((n, HY_W), F32),
        compiler_params=_cparams(1),
        name="hyena_filter",
    )(z, *consts)


FFT_N2 = 128


def _fft_tables(length):
    n = 2 * length
    n2 = FFT_N2
    n1 = n // n2
    k = np.arange(n1)
    idx = (n2 * np.outer(k, np.arange(n1))[None] + np.arange(n2)[:, None, None] * k[None, :, None]) % n
    ang = -2.0 * math.pi * idx / n
    stage1 = np.concatenate([np.cos(ang), np.sin(ang)], axis=1)
    ang2 = -2.0 * math.pi * (np.outer(np.arange(n2), np.arange(n2)) % n2) / n2
    fr, fi = np.cos(ang2), np.sin(ang2)
    stage2 = np.block([[fr, -fi], [fi, fr]])
    stage2_inv = np.block([[fr, fi], [-fi, fr]])
    angt = np.transpose(ang, (0, 2, 1))[:, :n1 // 2, :]
    stage3 = np.concatenate([np.cos(angt), np.sin(angt)], axis=2) / n
    as_bf16 = lambda a: jnp.asarray(a, F32).astype(BF16)
    return dict(stage1=as_bf16(stage1), stage2=as_bf16(stage2), stage2_inv=as_bf16(stage2_inv),
                stage3=as_bf16(stage3))


def _fft_stage1_kernel(x_ref, f_ref, o_ref):
    n1 = o_ref.shape[1]
    for j in range(x_ref.shape[1]):
        a = jnp.dot(f_ref[j], x_ref[:, j, :].astype(BF16), preferred_element_type=F32)
        o_ref[0, :, j, :] = a[:n1]
        o_ref[1, :, j, :] = a[n1:]


def _fft_stage1(x4, table, tn2=8):
    bn, kk, n2, ch = x4.shape
    n1 = table.shape[1] // 2
    return pl.pallas_call(
        _fft_stage1_kernel,
        grid=(n2 // tn2, bn),
        in_specs=[pl.BlockSpec((None, kk, tn2, ch), lambda i, b: (b, 0, i, 0)),
                  pl.BlockSpec((tn2, 2 * n1, kk), lambda i, b: (i, 0, 0))],
        out_specs=pl.BlockSpec((None, 2, n1, tn2, ch), lambda i, b: (b, 0, 0, i, 0)),
        out_shape=jax.ShapeDtypeStruct((bn, 2, n1, n2, ch), F32),
        compiler_params=_cparams(2),
        name="fft_stage1",
    )(x4, table)


def _fft_stage2_kernel(a_ref, e_ref, *rest, with_filter):
    if with_filter:
        h_ref, ei_ref, o_ref = rest
    else:
        (o_ref,) = rest
    n2 = a_ref.shape[2]
    for j in range(a_ref.shape[1]):
        x = jnp.dot(e_ref[...], jnp.concatenate([a_ref[0, j], a_ref[1, j]], axis=0).astype(BF16),
                    preferred_element_type=F32)
        if with_filter:
            xr, xi = x[:n2], x[n2:]
            hr, hi = h_ref[0, j], h_ref[1, j]
            z = jnp.concatenate([xr * hr - xi * hi, xr * hi + xi * hr], axis=0)
            x = jnp.dot(ei_ref[...], z.astype(BF16), preferred_element_type=F32)
        o_ref[0, j] = x[:n2]
        o_ref[1, j] = x[n2:]


def _fft_stage2(a, tables, spectrum=None, tk1=4):
    bn, _, n1, n2, ch = a.shape
    tk1 = min(tk1, n1)
    blk = lambda bmap: pl.BlockSpec((None, 2, tk1, n2, ch), bmap)
    mat = pl.BlockSpec((2 * n2, 2 * n2), lambda i, b: (0, 0))
    data_map = lambda i, b: (b, 0, i, 0, 0)
    if spectrum is None:
        ins, specs = [a, tables["stage2"]], [blk(data_map), mat]
    else:
        ins = [a, tables["stage2"], spectrum, tables["stage2_inv"]]
        specs = [blk(data_map), mat, blk(lambda i, b: (0, 0, i, 0, 0)), mat]
    return pl.pallas_call(
        functools.partial(_fft_stage2_kernel, with_filter=spectrum is not None),
        grid=(n1 // tk1, bn),
        in_specs=specs,
        out_specs=blk(data_map),
        out_shape=jax.ShapeDtypeStruct(a.shape, F32),
        compiler_params=_cparams(2),
        name="fft_stage2",
    )(*ins)


def _fft_stage3_kernel(b_ref, g_ref, p_ref, x0_ref, bias_ref, o_ref):
    for j in range(p_ref.shape[1]):
        rhs = jnp.concatenate([b_ref[0, :, j, :], b_ref[1, :, j, :]], axis=0).astype(BF16)
        y = jnp.dot(g_ref[j], rhs, preferred_element_type=F32)
        o_ref[:, j, :] = x0_ref[:, j, :] * (y + p_ref[:, j, :] * bias_ref[...])


def _fft_stage3(bm, table, p4, x04, bias, tn2=8):
    bn, kk, n2, ch = p4.shape
    n1 = bm.shape[2]
    seq = pl.BlockSpec((None, kk, tn2, ch), lambda i, b: (b, 0, i, 0))
    return pl.pallas_call(
        _fft_stage3_kernel,
        grid=(n2 // tn2, bn),
        in_specs=[pl.BlockSpec((None, 2, n1, tn2, ch), lambda i, b: (b, 0, 0, i, 0)),
                  pl.BlockSpec((tn2, kk, 2 * n1), lambda i, b: (i, 0, 0)),
                  seq, seq, pl.BlockSpec((1, ch), lambda i, b: (0, 0))],
        out_specs=seq,
        out_shape=jax.ShapeDtypeStruct(p4.shape, F32),
        compiler_params=_cparams(2),
        name="fft_stage3",
    )(bm, table, p4, x04, bias.reshape(1, ch))


def _long_conv_gated(p, x0, filt, bias, tables):
    bn, length, ch = p.shape
    n2 = FFT_N2
    n1 = 2 * length // n2
    spectrum = _fft_stage2(_fft_stage1(filt.reshape(1, n1, n2, ch), tables["stage1"]), tables)
    p4 = p.reshape(bn, n1 // 2, n2, ch)
    a = _fft_stage1(p4, tables["stage1"][:, :, :n1 // 2])
    bm = _fft_stage2(a, tables, spectrum)
    y = _fft_stage3(bm, tables["stage3"], p4, x0.reshape(bn, n1 // 2, n2, ch), bias)
    return y.reshape(bn, length, ch)


def _merge_kernel(x_ref, g_ref, ya_ref, yb_ref, yc_ref, yd_ref, wg_ref, wb_ref, wo_ref, o_ref):
    x = x_ref[...]
    d = x.shape[1]
    xn = _rmsnorm(x, g_ref[...]).astype(BF16)
    merged = None
    for n, y_ref in enumerate((ya_ref, yb_ref, yc_ref, yd_ref)):
        gate = jax.nn.sigmoid(jnp.dot(xn, wg_ref[:, n * d:(n + 1) * d], preferred_element_type=F32))
        term = gate * _dot(y_ref[...], wb_ref[n])
        merged = term if merged is None else merged + term
    o_ref[...] = x + _dot(merged, wo_ref[...])


def _merge(x2, g, ys, w_gate, w_branch, w_out, tm=256):
    t, d = x2.shape
    tm = min(tm, t)
    row = lambda w: pl.BlockSpec((tm, w), lambda i: (i, 0))
    full = lambda a: pl.BlockSpec(a.shape, lambda i: (0,) * a.ndim)
    return pl.pallas_call(
        _merge_kernel,
        grid=(t // tm,),
        in_specs=[row(d), pl.BlockSpec((1, d), lambda i: (0, 0))] + [row(y.shape[1]) for y in ys]
        + [full(w_gate), full(w_branch), full(w_out)],
        out_specs=row(d),
        out_shape=jax.ShapeDtypeStruct((t, d), F32),
        compiler_params=_cparams(1),
        name="merge",
    )(x2, g.reshape(1, d), *ys, w_gate, w_branch, w_out)


def _xattn_kernel(x_ref, g_ref, kv_ref, wq_ref, wo_ref, o_ref):
    x = x_ref[...]
    d = x.shape[1]
    hd = d // XA_HEADS
    q = _dot(_rmsnorm(x, g_ref[...]), wq_ref[...])
    outs = []
    for h in range(XA_HEADS):
        k = kv_ref[:, h * hd:(h + 1) * hd]
        v = kv_ref[:, d + h * hd:d + (h + 1) * hd]
        s = _dot_nt(q[:, h * hd:(h + 1) * hd], k) * (hd ** -0.5)
        s = s - jnp.max(s, axis=-1, keepdims=True)
        e = jnp.exp(s)
        p = e / jnp.sum(e, axis=-1, keepdims=True)
        outs.append(_dot(p, v))
    o_ref[...] = x + _dot(jnp.concatenate(outs, axis=1), wo_ref[...])


def _xattn(x3, g, kv, wq, wo, tq=512):
    bn, s, d = x3.shape
    m = kv.shape[1]
    tq = min(tq, s)
    return pl.pallas_call(
        _xattn_kernel,
        grid=(bn, s // tq),
        in_specs=[pl.BlockSpec((None, tq, d), lambda b, i: (b, i, 0)),
                  pl.BlockSpec((1, d), lambda b, i: (0, 0)),
                  pl.BlockSpec((None, m, 2 * d), lambda b, i: (b, 0, 0)),
                  pl.BlockSpec(wq.shape, lambda b, i: (0, 0)),
                  pl.BlockSpec(wo.shape, lambda b, i: (0, 0))],
        out_specs=pl.BlockSpec((None, tq, d), lambda b, i: (b, i, 0)),
        out_shape=jax.ShapeDtypeStruct((bn, s, d), F32),
        compiler_params=_cparams(2),
        name="xattn",
    )(x3, g.reshape(1, d), kv, wq, wo)


def _router_kernel(x_ref, g_ref, wr_ref, aff_ref):
    xn = _rmsnorm(x_ref[...], g_ref[...])
    logits = jnp.dot(xn.astype(BF16), wr_ref[...], preferred_element_type=F32)
    lane = lax.broadcasted_iota(jnp.int32, logits.shape, 1)
    logits = jnp.where(lane < N_EXPERTS, logits, -jnp.inf)
    e = jnp.exp(logits - jnp.max(logits, axis=-1, keepdims=True))
    aff_ref[...] = e / jnp.sum(e, axis=-1, keepdims=True)


def _router(x2, g, wr_pad, tm=512):
    t, d = x2.shape
    tm = min(tm, t)
    return pl.pallas_call(
        _router_kernel,
        grid=(t // tm,),
        in_specs=[pl.BlockSpec((tm, d), lambda i: (i, 0)), pl.BlockSpec((1, d), lambda i: (0, 0)),
                  pl.BlockSpec(wr_pad.shape, lambda i: (0, 0))],
        out_specs=pl.BlockSpec((tm, LANES), lambda i: (i, 0)),
        out_shape=jax.ShapeDtypeStruct((t, LANES), F32),
        compiler_params=_cparams(1),
        name="router",
    )(x2, g.reshape(1, d), wr_pad)


def _expert_kernel(idx_ref, gate_ref, g_ref, wg_ref, wu_ref, wd_ref, x_hbm, acc_in_hbm, out_hbm,
                   xbuf, abuf, sems):
    del acc_in_hbm
    tm = xbuf.shape[0]

    def x_row(r):
        return pltpu.make_async_copy(x_hbm.at[pl.ds(idx_ref[0, r], 1)], xbuf.at[pl.ds(r, 1)], sems.at[0])

    def acc_row_in(r):
        return pltpu.make_async_copy(out_hbm.at[pl.ds(idx_ref[0, r], 1)], abuf.at[pl.ds(r, 1)], sems.at[1])

    def acc_row_out(r):
        return pltpu.make_async_copy(abuf.at[pl.ds(r, 1)], out_hbm.at[pl.ds(idx_ref[0, r], 1)], sems.at[2])

    def start_gather(r, carry):
        x_row(r).start()
        acc_row_in(r).start()
        return carry

    def start_scatter(r, carry):
        acc_row_out(r).start()
        return carry

    lax.fori_loop(0, tm, start_gather, 0, unroll=8)
    pltpu.make_async_copy(x_hbm.at[pl.ds(0, tm)], xbuf, sems.at[0]).wait()
    pltpu.make_async_copy(out_hbm.at[pl.ds(0, tm)], abuf, sems.at[1]).wait()
    xe = _rmsnorm(xbuf[...], g_ref[...]).astype(BF16)
    hdn = _silu(jnp.dot(xe, wg_ref[...], preferred_element_type=F32)) * jnp.dot(
        xe, wu_ref[...], preferred_element_type=F32)
    abuf[...] = abuf[...] + _dot(hdn, wd_ref[...]) * gate_ref[...]
    lax.fori_loop(0, tm, start_scatter, 0, unroll=8)
    pltpu.make_async_copy(abuf, out_hbm.at[pl.ds(0, tm)], sems.at[2]).wait()


def _experts(x2, g, idx, gate, w_gate, w_up, w_down, tm=256):
    t, d = x2.shape
    ne, cap = idx.shape
    tm = min(tm, cap)
    nt = cap // tm
    wspec = lambda a: pl.BlockSpec((None,) + a.shape[1:], lambda e, i: (e, 0, 0))
    return pl.pallas_call(
        _expert_kernel,
        grid=(ne, nt),
        in_specs=[pl.BlockSpec((None, 1, tm), lambda e, i: (e * nt + i, 0, 0), memory_space=pltpu.SMEM),
                  pl.BlockSpec((None, tm, 1), lambda e, i: (e, i, 0)),
                  pl.BlockSpec((1, d), lambda e, i: (0, 0)),
                  wspec(w_gate), wspec(w_up), wspec(w_down),
                  pl.BlockSpec(memory_space=pl.ANY), pl.BlockSpec(memory_space=pl.ANY)],
        out_specs=pl.BlockSpec(memory_space=pl.ANY),
        out_shape=jax.ShapeDtypeStruct((t, d), F32),
        scratch_shapes=[pltpu.VMEM((tm, d), F32), pltpu.VMEM((tm, d), F32), pltpu.SemaphoreType.DMA((3,))],
        input_output_aliases={7: 0},
        compiler_params=_cparams(2),
        name="experts",
    )(idx.reshape(ne * nt, 1, tm), gate[..., None], g.reshape(1, d), w_gate, w_up, w_down, x2, x2)


def _final_norm_kernel(x_ref, g_ref, o_ref):
    o_ref[...] = _rmsnorm(x_ref[...], g_ref[...])


def _final_norm(x2, g, tm=512):
    t, d = x2.shape
    tm = min(tm, t)
    return pl.pallas_call(
        _final_norm_kernel,
        grid=(t // tm,),
        in_specs=[pl.BlockSpec((tm, d), lambda i: (i, 0)), pl.BlockSpec((1, d), lambda i: (0, 0))],
        out_specs=pl.BlockSpec((tm, d), lambda i: (i, 0)),
        out_shape=jax.ShapeDtypeStruct((t, d), F32),
        compiler_params=_cparams(1),
        name="final_norm",
    )(x2, g.reshape(1, d))


def _pad_cols(w, n):
    return jnp.pad(w, ((0, 0), (0, n - w.shape[1])))


def _block_diag(w):
    h, blk, _ = w.shape
    eye = jnp.eye(h, dtype=w.dtype)
    return jnp.einsum("hij,hk->hikj", w, eye).reshape(h * blk, h * blk)


def _row128(v):
    return _pad_cols(v.reshape(1, -1), LANES)


def _prep_layer(p, l):
    d = p["w_in"].shape[1]
    w_in = p["w_in"][l]
    sizes = (512, 512, 256, 256, 512, 512, 32, 512, 768, 16, 1536, 4 * d)
    offs = np.concatenate([[0], np.cumsum(sizes)])
    seg = [w_in[:, offs[i]:offs[i + 1]] for i in range(len(sizes))]
    a_x, a_y, b_q, b_k, b_v, b_g, b_lr, c_z, c_xbc, c_dt, d_u, gates = seg
    w_a = jnp.concatenate([a_x, a_y], axis=1)
    w_b = jnp.concatenate([b_q, b_k, b_v, b_g, _pad_cols(b_lr[:, :GLA_RANK], LANES),
                           _pad_cols(b_lr[:, GLA_RANK:], LANES)], axis=1)
    w_c = jnp.concatenate([c_z, c_xbc, _pad_cols(c_dt, LANES)], axis=1)
    out = {
        "w_mix": [w.astype(BF16) for w in (w_a, w_b, w_c, d_u)],
        "w_gates": gates.astype(BF16),
        "lru_gate_w": jnp.stack([jnp.concatenate([_block_diag(p["lru_gate_w"][l, dd, 0]),
                                                  _block_diag(p["lru_gate_w"][l, dd, 1])], axis=1)
                                 for dd in range(2)]).astype(BF16),
        "lru_gate_b": p["lru_gate_b"][l].reshape(2, 1, 2 * LRU_W),
        "lru_lambda": p["lru_lambda"][l].reshape(2, 1, LRU_W),
        "lru_conv_w": p["lru_conv_w"][l],
        "lru_conv_b": p["lru_conv_b"][l].reshape(1, -1),
        "gla_wa": jnp.pad(p["gla_wa2"][l], ((0, 0), (0, LANES - GLA_RANK), (0, 0))).astype(BF16),
        "gla_ba": p["gla_ba"][l].reshape(2, 1, GLA_QK),
        "gla_norm": p["gla_norm"][l].reshape(1, GLA_DV),
        "ssd_conv_w": p["ssd_conv_w"][l],
        "ssd_conv_b": p["ssd_conv_b"][l].reshape(1, -1),
        "ssd_dt_bias": _row128(p["ssd_dt_bias"][l]),
        "ssd_a_log": _row128(p["ssd_a_log"][l]),
        "ssd_d": jnp.repeat(p["ssd_d"][l], SSD_P).reshape(1, SSD_W),
        "ssd_norm": p["ssd_norm"][l].reshape(1, SSD_W),
        "hy_conv_w": p["hy_conv_w"][l],
        "hy_conv_b": p["hy_conv_b"][l].reshape(1, -1),
        "w_branch": p["w_branch"][l].astype(BF16),
        "w_out": p["w_out"][l].astype(BF16),
        "xa_wq": p["xa_wq"][l].astype(BF16),
        "xa_wkv": p["xa_wkv"][l].astype(BF16),
        "xa_wo": p["xa_wo"][l].astype(BF16),
        "router_w": _pad_cols(p["router_w"][l], LANES).astype(BF16),
        "exp_w_gate": p["exp_w_gate"][l].astype(BF16),
        "exp_w_up": p["exp_w_up"][l].astype(BF16),
        "exp_w_down": p["exp_w_down"][l].astype(BF16),
    }
    out.update({k: p[k][l] for k in RAW_KEYS})
    return out


RAW_KEYS = ("norm_mix", "norm_xa", "norm_mem", "norm_ffn", "hy_w1", "hy_b1", "hy_w2", "hy_b2", "hy_w3", "hy_b3",
            "hy_w4", "hy_freq", "hy_decay", "hy_bias")


def _seq_tile(s):
    return min(256, s)


def _seq_consts(s):
    return dict(z=_hyena_positions(s), fft=_fft_tables(s))


def _hyena(u_d, q, consts, ts):
    x0, pg = _hyena_pre(u_d, q["hy_conv_w"], q["hy_conv_b"], ts)
    filt = _hyena_filters(consts["z"], q["hy_w1"], q["hy_b1"], q["hy_w2"], q["hy_b2"], q["hy_w3"], q["hy_b3"],
                          q["hy_w4"], q["hy_freq"], q["hy_decay"])
    return _long_conv_gated(pg, x0, filt, q["hy_bias"], consts["fft"])


def _xattn_layer(x3, mem, q):
    bn, m, d = mem.shape
    (kv,) = _norm_proj(mem.reshape(bn * m, d), q["norm_mem"], [q["xa_wkv"]], out_dtype=BF16)
    return _xattn(x3, q["norm_xa"], kv.reshape(bn, m, 2 * d), q["xa_wq"], q["xa_wo"])


def _moe_layer(x2, q):
    t, d = x2.shape
    cap = max(1, EC_CAPACITY * t // N_EXPERTS)
    aff = _router(x2, q["norm_ffn"], q["router_w"])
    gate, idx = lax.top_k(aff[:, :N_EXPERTS].T, cap)
    return _experts(x2, q["norm_ffn"], idx, gate, q["exp_w_gate"], q["exp_w_up"], q["exp_w_down"])


def _layer(x, mem, q, consts):
    bn, s, d = x.shape
    t = bn * s
    ts = _seq_tile(s)
    x2 = x.reshape(t, d)

    u_a, u_b, u_c, u_d = _norm_proj(x2, q["norm_mix"], q["w_mix"])
    y_a = _lru(u_a.reshape(bn, s, -1), q["lru_conv_w"], q["lru_conv_b"], q["lru_gate_w"], q["lru_gate_b"],
               q["lru_lambda"], ts)
    y_b = _gla(u_b.reshape(bn, s, -1), q["gla_wa"], q["gla_ba"], q["gla_norm"], ts)
    y_c = _ssd(u_c.reshape(bn, s, -1), q["ssd_conv_w"], q["ssd_conv_b"], q["ssd_dt_bias"], q["ssd_a_log"],
               q["ssd_d"], q["ssd_norm"], ts)
    y_d = _hyena(u_d.reshape(bn, s, -1), q, consts, ts)
    ys = [y.reshape(t, -1) for y in (y_a, y_b, y_c, y_d)]
    x2 = _merge(x2, q["norm_mix"], ys, q["w_gates"], q["w_branch"], q["w_out"])

    x3 = _xattn_layer(x2.reshape(bn, s, d), mem, q)

    return _moe_layer(x3.reshape(t, d), q).reshape(bn, s, d)


def kernel(x_prompt, x_sample, mem_prompt, mem_sample, norm_mix, w_in, lru_conv_w, lru_conv_b, lru_gate_w, lru_gate_b, lru_lambda, gla_wa2, gla_ba, gla_norm, ssd_conv_w, ssd_conv_b, ssd_dt_bias, ssd_a_log, ssd_d, ssd_norm, hy_conv_w, hy_conv_b, hy_w1, hy_b1, hy_w2, hy_b2, hy_w3, hy_b3, hy_w4, hy_freq, hy_decay, hy_bias, w_branch, w_out, norm_xa, norm_mem, xa_wq, xa_wkv, xa_wo, norm_ffn, router_w, exp_w_gate, exp_w_up, exp_w_down, final_norm):
    p = dict(norm_mix=norm_mix, w_in=w_in, lru_conv_w=lru_conv_w, lru_conv_b=lru_conv_b, lru_gate_w=lru_gate_w,
             lru_gate_b=lru_gate_b, lru_lambda=lru_lambda, gla_wa2=gla_wa2, gla_ba=gla_ba, gla_norm=gla_norm,
             ssd_conv_w=ssd_conv_w, ssd_conv_b=ssd_conv_b, ssd_dt_bias=ssd_dt_bias, ssd_a_log=ssd_a_log,
             ssd_d=ssd_d, ssd_norm=ssd_norm, hy_conv_w=hy_conv_w, hy_conv_b=hy_conv_b, hy_w1=hy_w1, hy_b1=hy_b1,
             hy_w2=hy_w2, hy_b2=hy_b2, hy_w3=hy_w3, hy_b3=hy_b3, hy_w4=hy_w4, hy_freq=hy_freq, hy_decay=hy_decay,
             hy_bias=hy_bias, w_branch=w_branch, w_out=w_out, norm_xa=norm_xa, norm_mem=norm_mem, xa_wq=xa_wq,
             xa_wkv=xa_wkv, xa_wo=xa_wo, norm_ffn=norm_ffn, router_w=router_w, exp_w_gate=exp_w_gate,
             exp_w_up=exp_w_up, exp_w_down=exp_w_down)
    depth = w_in.shape[0]
    stacked = jax.tree.map(lambda *a: jnp.stack(a), *[_prep_layer(p, l) for l in range(depth)])
    outs = []
    for x, mem in ((x_prompt, mem_prompt), (x_sample, mem_sample)):
        consts = _seq_consts(x.shape[1])
        x, _ = lax.scan(lambda xc, q, mem=mem, consts=consts: (_layer(xc, mem, q, consts), None), x, stacked)
        bn, s, d = x.shape
        outs.append(_final_norm(x.reshape(bn * s, d), final_norm).reshape(bn, s, d))
    return tuple(outs)
```

```python
import functools
import math

import jax
import jax.numpy as jnp
import numpy as np
from jax import lax
from jax.experimental import pallas as pl
from jax.experimental.pallas import tpu as pltpu

F32 = jnp.float32
BF16 = jnp.bfloat16

EPS = 1e-6
CHUNK = 64
HALO = 8
LANES = 128
VMEM_LIMIT_BYTES = 56 * 1024 * 1024

LRU_C = 8.0
GLA_HEADS, GLA_DK, GLA_DV, GLA_RANK, GLA_TAU = 4, 64, 128, 16, 16.0
SSD_HEADS, SSD_P, SSD_GROUPS, SSD_N = 8, 64, 2, 64
XA_HEADS = 4
N_EXPERTS, EC_CAPACITY = 16, 2
HY_EMB = 33
HY_BANDS = (HY_EMB - 1) // 2


def _cparams(ndims):
    return pltpu.CompilerParams(dimension_semantics=("arbitrary",) * ndims,
                                vmem_limit_bytes=VMEM_LIMIT_BYTES)


def _rmsnorm(x, g):
    xf = x.astype(F32)
    return xf * lax.rsqrt(jnp.mean(xf * xf, axis=-1, keepdims=True) + EPS) * g


def _softplus(x):
    return jnp.maximum(x, 0.0) + jnp.log1p(jnp.exp(-jnp.abs(x)))


def _silu(x):
    return x * jax.nn.sigmoid(x)


def _gelu_tanh(x):
    return 0.5 * x * (1.0 + jnp.tanh(math.sqrt(2.0 / math.pi) * (x + 0.044715 * (x * x * x))))


def _dot(a, b):
    return jnp.dot(a.astype(BF16), b.astype(BF16), preferred_element_type=F32)


def _dot_nt(a, b):
    return lax.dot_general(a.astype(BF16), b.astype(BF16), (((1,), (1,)), ((), ())),
                           preferred_element_type=F32)


def _dot_tn(a, b):
    return lax.dot_general(a.astype(BF16), b.astype(BF16), (((0,), (0,)), ((), ())),
                           preferred_element_type=F32)


def _split_hi_lo(x):
    hi = x.astype(BF16)
    lo = (x - hi.astype(F32)).astype(BF16)
    return hi, lo


def _dot_exact_rhs(sel, x):
    hi, lo = _split_hi_lo(x)
    sel = sel.astype(BF16)
    return (jnp.dot(sel, hi, preferred_element_type=F32) + jnp.dot(sel, lo, preferred_element_type=F32))


def _dot_exact_lhs(x, sel):
    hi, lo = _split_hi_lo(x)
    sel = sel.astype(BF16)
    return (jnp.dot(hi, sel, preferred_element_type=F32) + jnp.dot(lo, sel, preferred_element_type=F32))


def _transpose_exact(x):
    n = x.shape[1]
    eye = (lax.broadcasted_iota(jnp.int32, (n, n), 0) == lax.broadcasted_iota(jnp.int32, (n, n), 1)).astype(BF16)
    hi, lo = _split_hi_lo(x)
    dn = (((1,), (1,)), ((), ()))
    return (lax.dot_general(eye, hi, dn, preferred_element_type=F32)
            + lax.dot_general(eye, lo, dn, preferred_element_type=F32))


def _tri(n, rev):
    l = lax.broadcasted_iota(jnp.int32, (n, n), 0)
    s = lax.broadcasted_iota(jnp.int32, (n, n), 1)
    return (s >= l) if rev else (s <= l)


def _conv_rows(prev, main, nxt, w_ref, b, left, first, last):
    ts = main.shape[0]
    prev = jnp.where(first, 0.0, prev)
    nxt = jnp.where(last, 0.0, nxt)
    ext = jnp.concatenate([prev, main, nxt], axis=0)
    n = ext.shape[0]
    out = b
    for j in range(w_ref.shape[0]):
        sh = (left - j) % n
        rolled = ext if sh == 0 else pltpu.roll(ext, sh, 0)
        out = out + w_ref[j:j + 1, :] * rolled[HALO:HALO + ts]
    return out


def _scan_rows(a, b, rev, block):
    n = a.shape[0]
    pos = lax.broadcasted_iota(jnp.int32, a.shape, 0) & (block - 1)
    s = 1
    while s < block:
        if rev:
            a_sh, b_sh, ok = pltpu.roll(a, n - s, 0), pltpu.roll(b, n - s, 0), pos < block - s
        else:
            a_sh, b_sh, ok = pltpu.roll(a, s, 0), pltpu.roll(b, s, 0), pos >= s
        b = b + a * jnp.where(ok, b_sh, 0.0)
        a = a * jnp.where(ok, a_sh, 1.0)
        s *= 2
    return a, b


def _inproj_kernel(x_ref, g_ref, *refs):
    n = len(refs) // 2
    xn = _rmsnorm(x_ref[...], g_ref[...]).astype(BF16)
    for w_ref, o_ref in zip(refs[:n], refs[n:]):
        o_ref[...] = jnp.dot(xn, w_ref[...], preferred_element_type=F32).astype(o_ref.dtype)


def _norm_proj(x2, g, ws, out_dtype=F32, tm=256):
    t, d = x2.shape
    tm = min(tm, t)
    in_specs = [pl.BlockSpec((tm, d), lambda i: (i, 0)), pl.BlockSpec((1, d), lambda i: (0, 0))]
    in_specs += [pl.BlockSpec(w.shape, lambda i: (0, 0)) for w in ws]
    outs = pl.pallas_call(
        _inproj_kernel,
        grid=(t // tm,),
        in_specs=in_specs,
        out_specs=[pl.BlockSpec((tm, w.shape[1]), lambda i: (i, 0)) for w in ws],
        out_shape=[jax.ShapeDtypeStruct((t, w.shape[1]), out_dtype) for w in ws],
        compiler_params=_cparams(1),
        name="norm_proj",
    )(x2, g.reshape(1, d), *ws)
    return outs


def _seq_specs(ts, nc, width):
    per = ts // HALO
    nb = nc * per

    def tile(b, d, c):
        return c + d * (nc - 1 - 2 * c)

    main = pl.BlockSpec((None, ts, width), lambda b, d, c: (b, tile(b, d, c), 0))
    prev = pl.BlockSpec((None, HALO, width), lambda b, d, c: (b, jnp.maximum(tile(b, d, c) * per - 1, 0), 0))
    nxt = pl.BlockSpec((None, HALO, width), lambda b, d, c: (b, jnp.minimum((tile(b, d, c) + 1) * per, nb - 1), 0))
    return main, prev, nxt


def _out_spec(ts, nc, width):
    return pl.BlockSpec((None, ts, width), lambda b, d, c: (b, nc - 1 - d * c, 0))


def _dir_spec(shape):
    return pl.BlockSpec((None,) + tuple(shape), lambda b, d, c: (d,) + (0,) * len(shape))


def _const_spec(shape):
    return pl.BlockSpec(tuple(shape), lambda b, d, c: (0,) * len(shape))


LRU_W = 512
LRU_SCAN_BLOCK = 16


def _lru_kernel(u_ref, prev_ref, next_ref, cw_ref, cb_ref, gw_ref, gb_ref, lam_ref, o_ref, hf_ref, carry_ref,
                *, ts, nc):
    d = pl.program_id(1)
    c = pl.program_id(2)

    @pl.when(c == 0)
    def _():
        carry_ref[...] = jnp.zeros_like(carry_ref)

    def run(rev):
        tile = (nc - 1 - c) if rev else c
        xc = _conv_rows(prev_ref[:, :LRU_W], u_ref[:, :LRU_W], next_ref[:, :LRU_W], cw_ref, cb_ref[...], 2,
                        tile == 0, tile == nc - 1)
        gl = _dot(xc, gw_ref[...]) + gb_ref[...]
        r = jax.nn.sigmoid(gl[:, :LRU_W])
        i = jax.nn.sigmoid(gl[:, LRU_W:])
        log_a = -LRU_C * r * _softplus(-lam_ref[...])
        a = jnp.exp(log_a)
        b = jnp.sqrt(1.0 - jnp.exp(2.0 * log_a)) * (i * xc)
        a_cum, h_loc = _scan_rows(a, b, rev, LRU_SCAN_BLOCK)
        nblk = ts // LRU_SCAN_BLOCK
        carry = carry_ref[...]
        hs = {}
        for k in (range(nblk - 1, -1, -1) if rev else range(nblk)):
            rows_k = slice(k * LRU_SCAN_BLOCK, (k + 1) * LRU_SCAN_BLOCK)
            hk = h_loc[rows_k] + a_cum[rows_k] * carry
            carry = hk[0:1, :] if rev else hk[LRU_SCAN_BLOCK - 1:LRU_SCAN_BLOCK, :]
            hs[k] = hk
        carry_ref[...] = carry
        h = jnp.concatenate([hs[k] for k in range(nblk)], axis=0)
        rows = pl.ds(pl.multiple_of(tile * ts, ts), ts)
        if rev:
            o_ref[...] = (hf_ref[rows, :] + h) * _gelu_tanh(u_ref[:, LRU_W:])
        else:
            hf_ref[rows, :] = h

    pl.when(d == 0)(functools.partial(run, False))
    pl.when(d == 1)(functools.partial(run, True))


def _lru(u_a, conv_w, conv_b, gate_w, gate_b, lam, ts):
    bn, s, width = u_a.shape
    nc = s // ts
    main, prev, nxt = _seq_specs(ts, nc, width)
    return pl.pallas_call(
        functools.partial(_lru_kernel, ts=ts, nc=nc),
        grid=(bn, 2, nc),
        in_specs=[main, prev, nxt, _const_spec(conv_w.shape), _const_spec(conv_b.shape),
                  _dir_spec(gate_w.shape[1:]), _dir_spec(gate_b.shape[1:]), _dir_spec(lam.shape[1:])],
        out_specs=_out_spec(ts, nc, LRU_W),
        out_shape=jax.ShapeDtypeStruct((bn, s, LRU_W), F32),
        scratch_shapes=[pltpu.VMEM((s, LRU_W), F32), pltpu.VMEM((1, LRU_W), F32)],
        compiler_params=_cparams(3),
        name="rglru",
    )(u_a, u_a, u_a, conv_w, conv_b, gate_w, gate_b, lam)


GLA_QK = GLA_HEADS * GLA_DK
GLA_V = GLA_HEADS * GLA_DV
GLA_COLS = 2 * GLA_QK + 2 * GLA_V + 2 * LANES


def _gla_kernel(u_ref, wa_ref, ba_ref, ng_ref, o_ref, of_ref, st_ref, *, ts, nc):
    d = pl.program_id(1)
    c = pl.program_id(2)
    nsub = ts // CHUNK

    @pl.when(c == 0)
    def _():
        st_ref[...] = jnp.zeros_like(st_ref)

    def run(rev):
        tile = (nc - 1 - c) if rev else c
        tri = _tri(CHUNK, rev)
        ref_row = CHUNK // 2 if rev else CHUNK // 2 - 1
        last_row = 0 if rev else CHUNK - 1
        code0 = 2 * GLA_QK + 2 * GLA_V + (LANES if rev else 0)
        stacked = (GLA_HEADS * CHUNK, GLA_QK)
        head_mask = (lax.shift_right_logical(lax.broadcasted_iota(jnp.int32, stacked, 0), CHUNK.bit_length() - 1)
                     == lax.shift_right_logical(lax.broadcasted_iota(jnp.int32, stacked, 1), GLA_DK.bit_length() - 1))
        tri_heads = jnp.concatenate([tri] * GLA_HEADS, axis=0)

        stack = lambda x: jnp.where(head_mask, jnp.concatenate([x] * GLA_HEADS, axis=0), 0.0)
        order = list(range(nsub - 1, -1, -1)) if rev else list(range(nsub))

        logits = _dot(u_ref[:, code0:code0 + LANES], wa_ref[...]) + ba_ref[...]
        log_a = -_softplus(-logits) * (1.0 / GLA_TAU)
        work = {}
        for ci in order:
            rows = slice(ci * CHUNK, (ci + 1) * CHUNK)
            q = u_ref[rows, 0:GLA_QK] * (GLA_DK ** -0.5)
            k = u_ref[rows, GLA_QK:2 * GLA_QK]
            v = u_ref[rows, 2 * GLA_QK:2 * GLA_QK + GLA_V]
            b = _dot_exact_rhs(tri, log_a[rows])
            b_mid = b[ref_row:ref_row + 1, :]
            b_end = b[last_row:last_row + 1, :]
            sc = jnp.where(tri_heads, _dot_nt(stack(q * jnp.exp(b - b_mid)), k * jnp.exp(b_mid - b)), 0.0)
            o_intra = _dot(sc, v)
            v_heads = jnp.concatenate([v[:, h * GLA_DV:(h + 1) * GLA_DV] for h in range(GLA_HEADS)], axis=0)
            loc = _dot_tn(v_heads, stack(k * jnp.exp(b_end - b)))
            work[ci] = (o_intra, stack(q * jnp.exp(b)).astype(BF16), jnp.exp(b_end), loc)

        st = st_ref[...]
        outs = {}
        for ci in order:
            o_intra, qe, decay, loc = work[ci]
            o_inter = _dot_nt(qe, st)
            outs[ci] = jnp.concatenate(
                [o_intra[h * CHUNK:(h + 1) * CHUNK, h * GLA_DV:(h + 1) * GLA_DV] + o_inter[h * CHUNK:(h + 1) * CHUNK]
                 for h in range(GLA_HEADS)], axis=1)
            st = st * decay + loc
        st_ref[...] = st

        seq_rows = pl.ds(pl.multiple_of(tile * ts, ts), ts)
        o = jnp.concatenate([outs[ci] for ci in range(nsub)], axis=0)
        if rev:
            o = o + of_ref[seq_rows, :]
            g = u_ref[:, 2 * GLA_QK + GLA_V:2 * GLA_QK + 2 * GLA_V]
            parts = []
            for h in range(GLA_HEADS):
                vs = slice(h * GLA_DV, (h + 1) * GLA_DV)
                parts.append(_rmsnorm(o[:, vs], ng_ref[...]) * _silu(g[:, vs]))
            o_ref[...] = jnp.concatenate(parts, axis=1)
        else:
            of_ref[seq_rows, :] = o

    pl.when(d == 0)(functools.partial(run, False))
    pl.when(d == 1)(functools.partial(run, True))


def _gla(u_b, wa, ba, norm_g, ts):
    bn, s, width = u_b.shape
    nc = s // ts
    main, _, _ = _seq_specs(ts, nc, width)
    return pl.pallas_call(
        functools.partial(_gla_kernel, ts=ts, nc=nc),
        grid=(bn, 2, nc),
        in_specs=[main, _dir_spec(wa.shape[1:]), _dir_spec(ba.shape[1:]), _const_spec(norm_g.shape)],
        out_specs=_out_spec(ts, nc, GLA_V),
        out_shape=jax.ShapeDtypeStruct((bn, s, GLA_V), F32),
        scratch_shapes=[pltpu.VMEM((s, GLA_V), F32), pltpu.VMEM((GLA_DV, GLA_QK), F32)],
        compiler_params=_cparams(3),
        name="gla",
    )(u_b, wa, ba, norm_g)


SSD_W = 512
SSD_BC = SSD_GROUPS * SSD_N
SSD_CONV = SSD_W + 2 * SSD_BC
SSD_COLS = SSD_W + SSD_CONV + LANES
SSD_GW = SSD_W // SSD_GROUPS


def _ssd_kernel(u_ref, prev_ref, next_ref, cw_ref, cb_ref, dtb_ref, alog_ref, dsk_ref, ng_ref, o_ref,
                yf_ref, st_ref, *, ts, nc):
    d = pl.program_id(1)
    c = pl.program_id(2)
    nsub = ts // CHUNK
    xbc = slice(SSD_W, SSD_W + SSD_CONV)
    dtc = slice(SSD_W + SSD_CONV, SSD_COLS)

    @pl.when(c == 0)
    def _():
        st_ref[...] = jnp.zeros_like(st_ref)

    def run(rev):
        tile = (nc - 1 - c) if rev else c
        tri = _tri(CHUNK, rev)
        last_row = 0 if rev else CHUNK - 1
        xs = _silu(_conv_rows(prev_ref[:, xbc], u_ref[:, xbc], next_ref[:, xbc], cw_ref, cb_ref[...], 2,
                              tile == 0, tile == nc - 1))
        lane = lax.broadcasted_iota(jnp.int32, (1, LANES), 1)
        a_row = jnp.where(lane < 2 * SSD_HEADS, -jnp.exp(alog_ref[...]), 0.0)
        er = lax.broadcasted_iota(jnp.int32, (LANES, SSD_W), 0)
        ec = lax.broadcasted_iota(jnp.int32, (LANES, SSD_W), 1)
        expand = er == (lax.shift_right_logical(ec, SSD_P.bit_length() - 1) + (SSD_HEADS if rev else 0))
        h0 = SSD_HEADS if rev else 0
        order = list(range(nsub - 1, -1, -1)) if rev else list(range(nsub))
        rl = lax.broadcasted_iota(jnp.int32, (ts, ts), 0)
        rs = lax.broadcasted_iota(jnp.int32, (ts, ts), 1)
        shift = CHUNK.bit_length() - 1
        same_chunk = lax.shift_right_logical(rl, shift) == lax.shift_right_logical(rs, shift)
        tri_tile = same_chunk & ((rs >= rl) if rev else (rs <= rl))

        dt = _softplus(u_ref[:, dtc] + dtb_ref[...])
        cs = _dot_exact_rhs(tri_tile, dt * a_row)
        cs_t = _transpose_exact(cs)
        both = _dot_exact_lhs(jnp.concatenate([dt, cs], axis=0), expand)
        dt_e, cs_e = both[:ts], both[ts:]
        work = {}
        for ci in order:
            rows = slice(ci * CHUNK, (ci + 1) * CHUNK)
            x = xs[rows, 0:SSD_W]
            bm = xs[rows, SSD_W:SSD_W + SSD_BC]
            cm = xs[rows, SSD_W + SSD_BC:SSD_CONV]
            cs_c = cs_e[rows]
            cs_end = cs_c[last_row:last_row + 1, :]
            xdt = x * dt_e[rows]
            xdec = xdt * jnp.exp(cs_end - cs_c)
            intra, locs = [], []
            for g in range(SSD_GROUPS):
                ns = slice(g * SSD_N, (g + 1) * SSD_N)
                gs = slice(g * SSD_GW, (g + 1) * SSD_GW)
                cb = _dot_nt(cm[:, ns], bm[:, ns])
                for r in range(SSD_HEADS // SSD_GROUPS):
                    hh = g * (SSD_HEADS // SSD_GROUPS) + r
                    j = h0 + hh
                    seg = cs[rows, j:j + 1] - cs_t[j:j + 1, rows]
                    lmat = jnp.exp(jnp.where(tri, seg, -jnp.inf))
                    intra.append(_dot(cb * lmat, xdt[:, hh * SSD_P:(hh + 1) * SSD_P]))
                locs.append(_dot_tn(bm[:, ns], xdec[:, gs]))
            work[ci] = (jnp.concatenate(intra, axis=1), cm.astype(BF16), jnp.exp(cs_c), jnp.exp(cs_end),
                        jnp.concatenate(locs, axis=1))

        st = st_ref[...]
        ys = {}
        for ci in order:
            y_intra, cm, grow, decay, loc = work[ci]
            y_inter = jnp.concatenate(
                [_dot(cm[:, g * SSD_N:(g + 1) * SSD_N], st[:, g * SSD_GW:(g + 1) * SSD_GW])
                 for g in range(SSD_GROUPS)], axis=1)
            ys[ci] = y_intra + y_inter * grow
            st = st * decay + loc
        st_ref[...] = st

        y = jnp.concatenate([ys[ci] for ci in range(nsub)], axis=0)
        seq_rows = pl.ds(pl.multiple_of(tile * ts, ts), ts)
        if rev:
            y = y + yf_ref[seq_rows, :] + xs[:, 0:SSD_W] * dsk_ref[...]
            y = y * _silu(u_ref[:, 0:SSD_W])
            parts = []
            for g in range(SSD_GROUPS):
                gs = slice(g * SSD_GW, (g + 1) * SSD_GW)
                parts.append(_rmsnorm(y[:, gs], ng_ref[:, gs]))
            o_ref[...] = jnp.concatenate(parts, axis=1)
        else:
            yf_ref[seq_rows, :] = y

    pl.when(d == 0)(functools.partial(run, False))
    pl.when(d == 1)(functools.partial(run, True))


def _ssd(u_c, conv_w, conv_b, dt_bias, a_log, d_skip, norm_g, ts):
    bn, s, width = u_c.shape
    nc = s // ts
    main, prev, nxt = _seq_specs(ts, nc, width)
    consts = [conv_w, conv_b, dt_bias, a_log, d_skip, norm_g]
    return pl.pallas_call(
        functools.partial(_ssd_kernel, ts=ts, nc=nc),
        grid=(bn, 2, nc),
        in_specs=[main, prev, nxt] + [_const_spec(a.shape) for a in consts],
        out_specs=_out_spec(ts, nc, SSD_W),
        out_shape=jax.ShapeDtypeStruct((bn, s, SSD_W), F32),
        scratch_shapes=[pltpu.VMEM((s, SSD_W), F32), pltpu.VMEM((SSD_N, SSD_W), F32)],
        compiler_params=_cparams(3),
        name="ssd",
    )(u_c, u_c, u_c, *consts)


HY_W = 512


def _hyena_pre_kernel(u_ref, prev_ref, next_ref, cw_ref, cb_ref, x0_ref, p_ref, *, nc):
    c = pl.program_id(1)
    uc = _conv_rows(prev_ref[...], u_ref[...], next_ref[...], cw_ref, cb_ref[...], 1, c == 0, c == nc - 1)
    x0_ref[...] = uc[:, 0:HY_W]
    p_ref[...] = uc[:, 2 * HY_W:3 * HY_W] * uc[:, HY_W:2 * HY_W]


def _hyena_pre(u_d, conv_w, conv_b, ts):
    bn, s, width = u_d.shape
    nc = s // ts
    per = ts // HALO
    nb = nc * per
    main = pl.BlockSpec((None, ts, width), lambda b, c: (b, c, 0))
    prev = pl.BlockSpec((None, HALO, width), lambda b, c: (b, jnp.maximum(c * per - 1, 0), 0))
    nxt = pl.BlockSpec((None, HALO, width), lambda b, c: (b, jnp.minimum((c + 1) * per, nb - 1), 0))
    out = pl.BlockSpec((None, ts, HY_W), lambda b, c: (b, c, 0))
    return pl.pallas_call(
        functools.partial(_hyena_pre_kernel, nc=nc),
        grid=(bn, nc),
        in_specs=[main, prev, nxt, pl.BlockSpec(conv_w.shape, lambda b, c: (0, 0)),
                  pl.BlockSpec(conv_b.shape, lambda b, c: (0, 0))],
        out_specs=[out, out],
        out_shape=[jax.ShapeDtypeStruct((bn, s, HY_W), F32)] * 2,
        compiler_params=_cparams(2),
        name="hyena_pre",
    )(u_d, u_d, u_d, conv_w, conv_b)


def _hyena_positions(length):
    t = np.linspace(0.0, 1.0, length)[:, None]
    w = (2.0 * math.pi / length) * np.arange(length)[:, None]
    f = np.linspace(1e-4, HY_BANDS - 1, HY_BANDS)[None, :]
    z = np.concatenate([t, np.cos(f * w), -np.sin(f * w)], axis=-1)
    z = np.concatenate([z, z[:1], z[:0:-1]], axis=0)
    return jnp.asarray(np.pad(z, ((0, 0), (0, LANES - HY_EMB))), F32)


def _hyena_filter_kernel(z_ref, w1_ref, b1_ref, w2_ref, b2_ref, w3_ref, b3_ref, w4_ref, fr_ref, dc_ref, o_ref):
    i = pl.program_id(0)
    half = pl.num_programs(0) // 2
    z = z_ref[...]
    h = jnp.sin(fr_ref[0:1, :] * (_dot(z, w1_ref[...]) + b1_ref[...]))
    h = jnp.sin(fr_ref[1:2, :] * (_dot(h, w2_ref[...]) + b2_ref[...]))
    h = jnp.sin(fr_ref[2:3, :] * (_dot(h, w3_ref[...]) + b3_ref[...]))
    banks = _dot(h, w4_ref[...]) * jnp.exp(-z[:, 0:1] * jnp.abs(dc_ref[...]))
    out = jnp.where(i < half, banks[:, :HY_W], banks[:, HY_W:])
    row = lax.broadcasted_iota(jnp.int32, out.shape, 0)
    o_ref[...] = jnp.where((i == half) & (row == 0), 0.0, out)


def _hyena_filters(z, w1, b1, w2, b2, w3, b3, w4, freq, decay, tl=512):
    n = z.shape[0]
    tl = min(tl, n // 2)
    consts = [jnp.pad(w1, ((0, LANES - HY_EMB), (0, 0))), b1.reshape(1, -1), w2, b2.reshape(1, -1), w3,
              b3.reshape(1, -1), w4, freq, decay.reshape(1, -1)]
    return pl.pallas_call(
        _hyena_filter_kernel,
        grid=(n // tl,),
        in_specs=[pl.BlockSpec((tl, LANES), lambda i: (i, 0))]
        + [pl.BlockSpec(a.shape, lambda i: (0, 0)) for a in consts],
        out_specs=pl.BlockSpec((tl, HY_W), lambda i: (i, 0)),
        out_shape=jax.ShapeDtypeStruct((n, HY_W), F32),
        compiler_params=_cparams(1),
        name="hyena_filter",
    )(z, *consts)


FFT_N2 = 128


def _fft_tables(length):
    n = 2 * length
    n2 = FFT_N2
    n1 = n // n2
    k = np.arange(n1)
    idx = (n2 * np.outer(k, np.arange(n1))[None] + np.arange(n2)[:, None, None] * k[None, :, None]) % n
    ang = -2.0 * math.pi * idx / n
    stage1 = np.concatenate([np.cos(ang), np.sin(ang)], axis=1)
    ang2 = -2.0 * math.pi * (np.outer(np.arange(n2), np.arange(n2)) % n2) / n2
    fr, fi = np.cos(ang2), np.sin(ang2)
    stage2 = np.block([[fr, -fi], [fi, fr]])
    stage2_inv = np.block([[fr, fi], [-fi, fr]])
    angt = np.transpose(ang, (0, 2, 1))[:, :n1 // 2, :]
    stage3 = np.concatenate([np.cos(angt), np.sin(angt)], axis=2) / n
    as_bf16 = lambda a: jnp.asarray(a, F32).astype(BF16)
    return dict(stage1=as_bf16(stage1), stage2=as_bf16(stage2), stage2_inv=as_bf16(stage2_inv),
                stage3=as_bf16(stage3))


def _fft_stage1_kernel(x_ref, f_ref, o_ref):
    n1 = o_ref.shape[1]
    for j in range(x_ref.shape[1]):
        a = jnp.dot(f_ref[j], x_ref[:, j, :].astype(BF16), preferred_element_type=F32)
        o_ref[0, :, j, :] = a[:n1]
        o_ref[1, :, j, :] = a[n1:]


def _fft_stage1(x4, table, tn2=8):
    bn, kk, n2, ch = x4.shape
    n1 = table.shape[1] // 2
    return pl.pallas_call(
        _fft_stage1_kernel,
        grid=(n2 // tn2, bn),
        in_specs=[pl.BlockSpec((None, kk, tn2, ch), lambda i, b: (b, 0, i, 0)),
                  pl.BlockSpec((tn2, 2 * n1, kk), lambda i, b: (i, 0, 0))],
        out_specs=pl.BlockSpec((None, 2, n1, tn2, ch), lambda i, b: (b, 0, 0, i, 0)),
        out_shape=jax.ShapeDtypeStruct((bn, 2, n1, n2, ch), F32),
        compiler_params=_cparams(2),
        name="fft_stage1",
    )(x4, table)


def _fft_stage2_kernel(a_ref, e_ref, *rest, with_filter):
    if with_filter:
        h_ref, ei_ref, o_ref = rest
    else:
        (o_ref,) = rest
    n2 = a_ref.shape[2]
    for j in range(a_ref.shape[1]):
        x = jnp.dot(e_ref[...], jnp.concatenate([a_ref[0, j], a_ref[1, j]], axis=0).astype(BF16),
                    preferred_element_type=F32)
        if with_filter:
            xr, xi = x[:n2], x[n2:]
            hr, hi = h_ref[0, j], h_ref[1, j]
            z = jnp.concatenate([xr * hr - xi * hi, xr * hi + xi * hr], axis=0)
            x = jnp.dot(ei_ref[...], z.astype(BF16), preferred_element_type=F32)
        o_ref[0, j] = x[:n2]
        o_ref[1, j] = x[n2:]


def _fft_stage2(a, tables, spectrum=None, tk1=4):
    bn, _, n1, n2, ch = a.shape
    tk1 = min(tk1, n1)
    blk = lambda bmap: pl.BlockSpec((None, 2, tk1, n2, ch), bmap)
    mat = pl.BlockSpec((2 * n2, 2 * n2), lambda i, b: (0, 0))
    data_map = lambda i, b: (b, 0, i, 0, 0)
    if spectrum is None:
        ins, specs = [a, tables["stage2"]], [blk(data_map), mat]
    else:
        ins = [a, tables["stage2"], spectrum, tables["stage2_inv"]]
        specs = [blk(data_map), mat, blk(lambda i, b: (0, 0, i, 0, 0)), mat]
    return pl.pallas_call(
        functools.partial(_fft_stage2_kernel, with_filter=spectrum is not None),
        grid=(n1 // tk1, bn),
        in_specs=specs,
        out_specs=blk(data_map),
        out_shape=jax.ShapeDtypeStruct(a.shape, F32),
        compiler_params=_cparams(2),
        name="fft_stage2",
    )(*ins)


def _fft_stage3_kernel(b_ref, g_ref, p_ref, x0_ref, bias_ref, o_ref):
    for j in range(p_ref.shape[1]):
        rhs = jnp.concatenate([b_ref[0, :, j, :], b_ref[1, :, j, :]], axis=0).astype(BF16)
        y = jnp.dot(g_ref[j], rhs, preferred_element_type=F32)
        o_ref[:, j, :] = x0_ref[:, j, :] * (y + p_ref[:, j, :] * bias_ref[...])


def _fft_stage3(bm, table, p4, x04, bias, tn2=8):
    bn, kk, n2, ch = p4.shape
    n1 = bm.shape[2]
    seq = pl.BlockSpec((None, kk, tn2, ch), lambda i, b: (b, 0, i, 0))
    return pl.pallas_call(
        _fft_stage3_kernel,
        grid=(n2 // tn2, bn),
        in_specs=[pl.BlockSpec((None, 2, n1, tn2, ch), lambda i, b: (b, 0, 0, i, 0)),
                  pl.BlockSpec((tn2, kk, 2 * n1), lambda i, b: (i, 0, 0)),
                  seq, seq, pl.BlockSpec((1, ch), lambda i, b: (0, 0))],
        out_specs=seq,
        out_shape=jax.ShapeDtypeStruct(p4.shape, F32),
        compiler_params=_cparams(2),
        name="fft_stage3",
    )(bm, table, p4, x04, bias.reshape(1, ch))


def _long_conv_gated(p, x0, filt, bias, tables):
    bn, length, ch = p.shape
    n2 = FFT_N2
    n1 = 2 * length // n2
    spectrum = _fft_stage2(_fft_stage1(filt.reshape(1, n1, n2, ch), tables["stage1"]), tables)
    p4 = p.reshape(bn, n1 // 2, n2, ch)
    a = _fft_stage1(p4, tables["stage1"][:, :, :n1 // 2])
    bm = _fft_stage2(a, tables, spectrum)
    y = _fft_stage3(bm, tables["stage3"], p4, x0.reshape(bn, n1 // 2, n2, ch), bias)
    return y.reshape(bn, length, ch)


def _merge_kernel(x_ref, g_ref, ya_ref, yb_ref, yc_ref, yd_ref, wg_ref, wb_ref, wo_ref, o_ref):
    x = x_ref[...]
    d = x.shape[1]
    xn = _rmsnorm(x, g_ref[...]).astype(BF16)
    merged = None
    for n, y_ref in enumerate((ya_ref, yb_ref, yc_ref, yd_ref)):
        gate = jax.nn.sigmoid(jnp.dot(xn, wg_ref[:, n * d:(n + 1) * d], preferred_element_type=F32))
        term = gate * _dot(y_ref[...], wb_ref[n])
        merged = term if merged is None else merged + term
    o_ref[...] = x + _dot(merged, wo_ref[...])


def _merge(x2, g, ys, w_gate, w_branch, w_out, tm=512):
    t, d = x2.shape
    tm = min(tm, t)
    row = lambda w: pl.BlockSpec((tm, w), lambda i: (i, 0))
    full = lambda a: pl.BlockSpec(a.shape, lambda i: (0,) * a.ndim)
    return pl.pallas_call(
        _merge_kernel,
        grid=(t // tm,),
        in_specs=[row(d), pl.BlockSpec((1, d), lambda i: (0, 0))] + [row(y.shape[1]) for y in ys]
        + [full(w_gate), full(w_branch), full(w_out)],
        out_specs=row(d),
        out_shape=jax.ShapeDtypeStruct((t, d), F32),
        compiler_params=_cparams(1),
        name="merge",
    )(x2, g.reshape(1, d), *ys, w_gate, w_branch, w_out)


def _xattn_kernel(x_ref, g_ref, kv_ref, wq_ref, wo_ref, o_ref):
    x = x_ref[...]
    d = x.shape[1]
    hd = d // XA_HEADS
    q = _dot(_rmsnorm(x, g_ref[...]), wq_ref[...])
    outs = []
    for h in range(XA_HEADS):
        k = kv_ref[:, h * hd:(h + 1) * hd]
        v = kv_ref[:, d + h * hd:d + (h + 1) * hd]
        s = _dot_nt(q[:, h * hd:(h + 1) * hd], k) * (hd ** -0.5)
        s = s - jnp.max(s, axis=-1, keepdims=True)
        e = jnp.exp(s)
        p = e / jnp.sum(e, axis=-1, keepdims=True)
        outs.append(_dot(p, v))
    o_ref[...] = x + _dot(jnp.concatenate(outs, axis=1), wo_ref[...])


def _xattn(x3, g, kv, wq, wo, tq=512):
    bn, s, d = x3.shape
    m = kv.shape[1]
    tq = min(tq, s)
    return pl.pallas_call(
        _xattn_kernel,
        grid=(bn, s // tq),
        in_specs=[pl.BlockSpec((None, tq, d), lambda b, i: (b, i, 0)),
                  pl.BlockSpec((1, d), lambda b, i: (0, 0)),
                  pl.BlockSpec((None, m, 2 * d), lambda b, i: (b, 0, 0)),
                  pl.BlockSpec(wq.shape, lambda b, i: (0, 0)),
                  pl.BlockSpec(wo.shape, lambda b, i: (0, 0))],
        out_specs=pl.BlockSpec((None, tq, d), lambda b, i: (b, i, 0)),
        out_shape=jax.ShapeDtypeStruct((bn, s, d), F32),
        compiler_params=_cparams(2),
        name="xattn",
    )(x3, g.reshape(1, d), kv, wq, wo)


def _router_kernel(x_ref, g_ref, wr_ref, aff_ref):
    xn = _rmsnorm(x_ref[...], g_ref[...])
    logits = jnp.dot(xn.astype(BF16), wr_ref[...], preferred_element_type=F32)
    lane = lax.broadcasted_iota(jnp.int32, logits.shape, 1)
    logits = jnp.where(lane < N_EXPERTS, logits, -jnp.inf)
    e = jnp.exp(logits - jnp.max(logits, axis=-1, keepdims=True))
    aff_ref[...] = e / jnp.sum(e, axis=-1, keepdims=True)


def _router(x2, g, wr_pad, tm=512):
    t, d = x2.shape
    tm = min(tm, t)
    return pl.pallas_call(
        _router_kernel,
        grid=(t // tm,),
        in_specs=[pl.BlockSpec((tm, d), lambda i: (i, 0)), pl.BlockSpec((1, d), lambda i: (0, 0)),
                  pl.BlockSpec(wr_pad.shape, lambda i: (0, 0))],
        out_specs=pl.BlockSpec((tm, LANES), lambda i: (i, 0)),
        out_shape=jax.ShapeDtypeStruct((t, LANES), F32),
        compiler_params=_cparams(1),
        name="router",
    )(x2, g.reshape(1, d), wr_pad)


def _expert_kernel(idx_ref, gate_ref, g_ref, wg_ref, wu_ref, wd_ref, x_hbm, acc_in_hbm, out_hbm,
                   xbuf, abuf, sems):
    del acc_in_hbm
    tm = xbuf.shape[0]

    def x_row(r):
        return pltpu.make_async_copy(x_hbm.at[pl.ds(idx_ref[0, r], 1)], xbuf.at[pl.ds(r, 1)], sems.at[0])

    def acc_row_in(r):
        return pltpu.make_async_copy(out_hbm.at[pl.ds(idx_ref[0, r], 1)], abuf.at[pl.ds(r, 1)], sems.at[1])

    def acc_row_out(r):
        return pltpu.make_async_copy(abuf.at[pl.ds(r, 1)], out_hbm.at[pl.ds(idx_ref[0, r], 1)], sems.at[2])

    def start_gather(r, carry):
        x_row(r).start()
        acc_row_in(r).start()
        return carry

    def start_scatter(r, carry):
        acc_row_out(r).start()
        return carry

    lax.fori_loop(0, tm, start_gather, 0, unroll=8)
    pltpu.make_async_copy(x_hbm.at[pl.ds(0, tm)], xbuf, sems.at[0]).wait()
    xe = _rmsnorm(xbuf[...], g_ref[...]).astype(BF16)
    hdn = _silu(jnp.dot(xe, wg_ref[...], preferred_element_type=F32)) * jnp.dot(
        xe, wu_ref[...], preferred_element_type=F32)
    ye = _dot(hdn, wd_ref[...]) * gate_ref[...]
    pltpu.make_async_copy(out_hbm.at[pl.ds(0, tm)], abuf, sems.at[1]).wait()
    abuf[...] = abuf[...] + ye
    lax.fori_loop(0, tm, start_scatter, 0, unroll=8)
    pltpu.make_async_copy(abuf, out_hbm.at[pl.ds(0, tm)], sems.at[2]).wait()


def _experts(x2, g, idx, gate, w_gate, w_up, w_down, tm=512):
    t, d = x2.shape
    ne, cap = idx.shape
    tm = min(tm, cap)
    nt = cap // tm
    wspec = lambda a: pl.BlockSpec((None,) + a.shape[1:], lambda e, i: (e, 0, 0))
    return pl.pallas_call(
        _expert_kernel,
        grid=(ne, nt),
        in_specs=[pl.BlockSpec((None, 1, tm), lambda e, i: (e * nt + i, 0, 0), memory_space=pltpu.SMEM),
                  pl.BlockSpec((None, tm, 1), lambda e, i: (e, i, 0)),
                  pl.BlockSpec((1, d), lambda e, i: (0, 0)),
                  wspec(w_gate), wspec(w_up), wspec(w_down),
                  pl.BlockSpec(memory_space=pl.ANY), pl.BlockSpec(memory_space=pl.ANY)],
        out_specs=pl.BlockSpec(memory_space=pl.ANY),
        out_shape=jax.ShapeDtypeStruct((t, d), F32),
        scratch_shapes=[pltpu.VMEM((tm, d), F32), pltpu.VMEM((tm, d), F32), pltpu.SemaphoreType.DMA((3,))],
        input_output_aliases={7: 0},
        compiler_params=_cparams(2),
        name="experts",
    )(idx.reshape(ne * nt, 1, tm), gate[..., None], g.reshape(1, d), w_gate, w_up, w_down, x2, x2)


def _final_norm_kernel(x_ref, g_ref, o_ref):
    o_ref[...] = _rmsnorm(x_ref[...], g_ref[...])


def _final_norm(x2, g, tm=512):
    t, d = x2.shape
    tm = min(tm, t)
    return pl.pallas_call(
        _final_norm_kernel,
        grid=(t // tm,),
        in_specs=[pl.BlockSpec((tm, d), lambda i: (i, 0)), pl.BlockSpec((1, d), lambda i: (0, 0))],
        out_specs=pl.BlockSpec((tm, d), lambda i: (i, 0)),
        out_shape=jax.ShapeDtypeStruct((t, d), F32),
        compiler_params=_cparams(1),
        name="final_norm",
    )(x2, g.reshape(1, d))


def _pad_cols(w, n):
    return jnp.pad(w, ((0, 0), (0, n - w.shape[1])))


def _block_diag(w):
    h, blk, _ = w.shape
    eye = jnp.eye(h, dtype=w.dtype)
    return jnp.einsum("hij,hk->hikj", w, eye).reshape(h * blk, h * blk)


def _row128(v):
    return _pad_cols(v.reshape(1, -1), LANES)


def _prep_layer(p, l):
    d = p["w_in"].shape[1]
    w_in = p["w_in"][l]
    sizes = (512, 512, 256, 256, 512, 512, 32, 512, 768, 16, 1536, 4 * d)
    offs = np.concatenate([[0], np.cumsum(sizes)])
    seg = [w_in[:, offs[i]:offs[i + 1]] for i in range(len(sizes))]
    a_x, a_y, b_q, b_k, b_v, b_g, b_lr, c_z, c_xbc, c_dt, d_u, gates = seg
    w_a = jnp.concatenate([a_x, a_y], axis=1)
    w_b = jnp.concatenate([b_q, b_k, b_v, b_g, _pad_cols(b_lr[:, :GLA_RANK], LANES),
                           _pad_cols(b_lr[:, GLA_RANK:], LANES)], axis=1)
    w_c = jnp.concatenate([c_z, c_xbc, _pad_cols(c_dt, LANES)], axis=1)
    out = {
        "w_mix": [w.astype(BF16) for w in (w_a, w_b, w_c, d_u)],
        "w_gates": gates.astype(BF16),
        "lru_gate_w": jnp.stack([jnp.concatenate([_block_diag(p["lru_gate_w"][l, dd, 0]),
                                                  _block_diag(p["lru_gate_w"][l, dd, 1])], axis=1)
                                 for dd in range(2)]).astype(BF16),
        "lru_gate_b": p["lru_gate_b"][l].reshape(2, 1, 2 * LRU_W),
        "lru_lambda": p["lru_lambda"][l].reshape(2, 1, LRU_W),
        "lru_conv_w": p["lru_conv_w"][l],
        "lru_conv_b": p["lru_conv_b"][l].reshape(1, -1),
        "gla_wa": jnp.pad(p["gla_wa2"][l], ((0, 0), (0, LANES - GLA_RANK), (0, 0))).astype(BF16),
        "gla_ba": p["gla_ba"][l].reshape(2, 1, GLA_QK),
        "gla_norm": p["gla_norm"][l].reshape(1, GLA_DV),
        "ssd_conv_w": p["ssd_conv_w"][l],
        "ssd_conv_b": p["ssd_conv_b"][l].reshape(1, -1),
        "ssd_dt_bias": _row128(p["ssd_dt_bias"][l]),
        "ssd_a_log": _row128(p["ssd_a_log"][l]),
        "ssd_d": jnp.repeat(p["ssd_d"][l], SSD_P).reshape(1, SSD_W),
        "ssd_norm": p["ssd_norm"][l].reshape(1, SSD_W),
        "hy_conv_w": p["hy_conv_w"][l],
        "hy_conv_b": p["hy_conv_b"][l].reshape(1, -1),
        "w_branch": p["w_branch"][l].astype(BF16),
        "w_out": p["w_out"][l].astype(BF16),
        "xa_wq": p["xa_wq"][l].astype(BF16),
        "xa_wkv": p["xa_wkv"][l].astype(BF16),
        "xa_wo": p["xa_wo"][l].astype(BF16),
        "router_w": _pad_cols(p["router_w"][l], LANES).astype(BF16),
        "exp_w_gate": p["exp_w_gate"][l].astype(BF16),
        "exp_w_up": p["exp_w_up"][l].astype(BF16),
        "exp_w_down": p["exp_w_down"][l].astype(BF16),
    }
    out.update({k: p[k][l] for k in RAW_KEYS})
    return out


RAW_KEYS = ("norm_mix", "norm_xa", "norm_mem", "norm_ffn", "hy_w1", "hy_b1", "hy_w2", "hy_b2", "hy_w3", "hy_b3",
            "hy_w4", "hy_freq", "hy_decay", "hy_bias")


def _seq_tile(s):
    return min(256, s)


def _seq_consts(s):
    return dict(z=_hyena_positions(s), fft=_fft_tables(s))


def _hyena(u_d, q, consts, ts):
    x0, pg = _hyena_pre(u_d, q["hy_conv_w"], q["hy_conv_b"], ts)
    filt = _hyena_filters(consts["z"], q["hy_w1"], q["hy_b1"], q["hy_w2"], q["hy_b2"], q["hy_w3"], q["hy_b3"],
                          q["hy_w4"], q["hy_freq"], q["hy_decay"])
    return _long_conv_gated(pg, x0, filt, q["hy_bias"], consts["fft"])


def _xattn_layer(x3, mem, q):
    bn, m, d = mem.shape
    (kv,) = _norm_proj(mem.reshape(bn * m, d), q["norm_mem"], [q["xa_wkv"]], out_dtype=BF16)
    return _xattn(x3, q["norm_xa"], kv.reshape(bn, m, 2 * d), q["xa_wq"], q["xa_wo"])


def _moe_layer(x2, q):
    t, d = x2.shape
    cap = max(1, EC_CAPACITY * t // N_EXPERTS)
    aff = _router(x2, q["norm_ffn"], q["router_w"])
    gate, idx = lax.top_k(aff[:, :N_EXPERTS].T, cap)
    return _experts(x2, q["norm_ffn"], idx, gate, q["exp_w_gate"], q["exp_w_up"], q["exp_w_down"])


def _layer(x, mem, q, consts):
    bn, s, d = x.shape
    t = bn * s
    ts = _seq_tile(s)
    x2 = x.reshape(t, d)

    u_a, u_b, u_c, u_d = _norm_proj(x2, q["norm_mix"], q["w_mix"])
    y_a = _lru(u_a.reshape(bn, s, -1), q["lru_conv_w"], q["lru_conv_b"], q["lru_gate_w"], q["lru_gate_b"],
               q["lru_lambda"], ts)
    y_b = _gla(u_b.reshape(bn, s, -1), q["gla_wa"], q["gla_ba"], q["gla_norm"], ts)
    y_c = _ssd(u_c.reshape(bn, s, -1), q["ssd_conv_w"], q["ssd_conv_b"], q["ssd_dt_bias"], q["ssd_a_log"],
               q["ssd_d"], q["ssd_norm"], ts)
    y_d = _hyena(u_d.reshape(bn, s, -1), q, consts, ts)
    ys = [y.reshape(t, -1) for y in (y_a, y_b, y_c, y_d)]
    x2 = _merge(x2, q["norm_mix"], ys, q["w_gates"], q["w_branch"], q["w_out"])

    x3 = _xattn_layer(x2.reshape(bn, s, d), mem, q)

    return _moe_layer(x3.reshape(t, d), q).reshape(bn, s, d)


def kernel(x_prompt, x_sample, mem_prompt, mem_sample, norm_mix, w_in, lru_conv_w, lru_conv_b, lru_gate_w, lru_gate_b, lru_lambda, gla_wa2, gla_ba, gla_norm, ssd_conv_w, ssd_conv_b, ssd_dt_bias, ssd_a_log, ssd_d, ssd_norm, hy_conv_w, hy_conv_b, hy_w1, hy_b1, hy_w2, hy_b2, hy_w3, hy_b3, hy_w4, hy_freq, hy_decay, hy_bias, w_branch, w_out, norm_xa, norm_mem, xa_wq, xa_wkv, xa_wo, norm_ffn, router_w, exp_w_gate, exp_w_up, exp_w_down, final_norm):
    p = dict(norm_mix=norm_mix, w_in=w_in, lru_conv_w=lru_conv_w, lru_conv_b=lru_conv_b, lru_gate_w=lru_gate_w,
             lru_gate_b=lru_gate_b, lru_lambda=lru_lambda, gla_wa2=gla_wa2, gla_ba=gla_ba, gla_norm=gla_norm,
             ssd_conv_w=ssd_conv_w, ssd_conv_b=ssd_conv_b, ssd_dt_bias=ssd_dt_bias, ssd_a_log=ssd_a_log,
             ssd_d=ssd_d, ssd_norm=ssd_norm, hy_conv_w=hy_conv_w, hy_conv_b=hy_conv_b, hy_w1=hy_w1, hy_b1=hy_b1,
             hy_w2=hy_w2, hy_b2=hy_b2, hy_w3=hy_w3, hy_b3=hy_b3, hy_w4=hy_w4, hy_freq=hy_freq, hy_decay=hy_decay,
             hy_bias=hy_bias, w_branch=w_branch, w_out=w_out, norm_xa=norm_xa, norm_mem=norm_mem, xa_wq=xa_wq,
             xa_wkv=xa_wkv, xa_wo=xa_wo, norm_ffn=norm_ffn, router_w=router_w, exp_w_gate=exp_w_gate,
             exp_w_up=exp_w_up, exp_w_down=exp_w_down)
    depth = w_in.shape[0]
    stacked = jax.tree.map(lambda *a: jnp.stack(a), *[_prep_layer(p, l) for l in range(depth)])
    outs = []
    for x, mem in ((x_prompt, mem_prompt), (x_sample, mem_sample)):
        consts = _seq_consts(x.shape[1])
        x, _ = lax.scan(lambda xc, q, mem=mem, consts=consts: (_layer(xc, mem, q, consts), None), x, stacked)
        bn, s, d = x.shape
        outs.append(_final_norm(x.reshape(bn * s, d), final_norm).reshape(bn, s, d))
    return tuple(outs)
```

```python
import functools
import math

import jax
import jax.numpy as jnp
import numpy as np
from jax import lax
from jax.experimental import pallas as pl
from jax.experimental.pallas import tpu as pltpu

F32 = jnp.float32
BF16 = jnp.bfloat16

EPS = 1e-6
CHUNK = 64
HALO = 8
LANES = 128
VMEM_LIMIT_BYTES = 56 * 1024 * 1024

LRU_C = 8.0
GLA_HEADS, GLA_DK, GLA_DV, GLA_RANK, GLA_TAU = 4, 64, 128, 16, 16.0
SSD_HEADS, SSD_P, SSD_GROUPS, SSD_N = 8, 64, 2, 64
XA_HEADS = 4
N_EXPERTS, EC_CAPACITY = 16, 2
HY_EMB = 33
HY_BANDS = (HY_EMB - 1) // 2


def _cparams(ndims):
    return pltpu.CompilerParams(dimension_semantics=("arbitrary",) * ndims,
                                vmem_limit_bytes=VMEM_LIMIT_BYTES)


def _rmsnorm(x, g):
    xf = x.astype(F32)
    return xf * lax.rsqrt(jnp.mean(xf * xf, axis=-1, keepdims=True) + EPS) * g


def _softplus(x):
    return jnp.maximum(x, 0.0) + jnp.log1p(jnp.exp(-jnp.abs(x)))


def _silu(x):
    return x * jax.nn.sigmoid(x)


def _gelu_tanh(x):
    return 0.5 * x * (1.0 + jnp.tanh(math.sqrt(2.0 / math.pi) * (x + 0.044715 * (x * x * x))))


def _dot(a, b):
    return jnp.dot(a.astype(BF16), b.astype(BF16), preferred_element_type=F32)


def _dot_nt(a, b):
    return lax.dot_general(a.astype(BF16), b.astype(BF16), (((1,), (1,)), ((), ())),
                           preferred_element_type=F32)


def _dot_tn(a, b):
    return lax.dot_general(a.astype(BF16), b.astype(BF16), (((0,), (0,)), ((), ())),
                           preferred_element_type=F32)


def _split_hi_lo(x):
    hi = x.astype(BF16)
    lo = (x - hi.astype(F32)).astype(BF16)
    return hi, lo


def _dot_exact_rhs(sel, x):
    hi, lo = _split_hi_lo(x)
    sel = sel.astype(BF16)
    return (jnp.dot(sel, hi, preferred_element_type=F32) + jnp.dot(sel, lo, preferred_element_type=F32))


def _dot_exact_lhs(x, sel):
    hi, lo = _split_hi_lo(x)
    sel = sel.astype(BF16)
    return (jnp.dot(hi, sel, preferred_element_type=F32) + jnp.dot(lo, sel, preferred_element_type=F32))


def _transpose_exact(x):
    n = x.shape[1]
    eye = (lax.broadcasted_iota(jnp.int32, (n, n), 0) == lax.broadcasted_iota(jnp.int32, (n, n), 1)).astype(BF16)
    hi, lo = _split_hi_lo(x)
    dn = (((1,), (1,)), ((), ()))
    return (lax.dot_general(eye, hi, dn, preferred_element_type=F32)
            + lax.dot_general(eye, lo, dn, preferred_element_type=F32))


def _tri(n, rev):
    l = lax.broadcasted_iota(jnp.int32, (n, n), 0)
    s = lax.broadcasted_iota(jnp.int32, (n, n), 1)
    return (s >= l) if rev else (s <= l)


def _conv_rows(prev, main, nxt, w_ref, b, left, first, last):
    ts = main.shape[0]
    prev = jnp.where(first, 0.0, prev)
    nxt = jnp.where(last, 0.0, nxt)
    ext = jnp.concatenate([prev, main, nxt], axis=0)
    n = ext.shape[0]
    out = b
    for j in range(w_ref.shape[0]):
        sh = (left - j) % n
        rolled = ext if sh == 0 else pltpu.roll(ext, sh, 0)
        out = out + w_ref[j:j + 1, :] * rolled[HALO:HALO + ts]
    return out


def _scan_rows(a, b, rev, block):
    n = a.shape[0]
    pos = lax.broadcasted_iota(jnp.int32, a.shape, 0) & (block - 1)
    s = 1
    while s < block:
        if rev:
            a_sh, b_sh, ok = pltpu.roll(a, n - s, 0), pltpu.roll(b, n - s, 0), pos < block - s
        else:
            a_sh, b_sh, ok = pltpu.roll(a, s, 0), pltpu.roll(b, s, 0), pos >= s
        b = b + a * jnp.where(ok, b_sh, 0.0)
        a = a * jnp.where(ok, a_sh, 1.0)
        s *= 2
    return a, b


def _inproj_kernel(x_ref, g_ref, *refs):
    n = len(refs) // 2
    xn = _rmsnorm(x_ref[...], g_ref[...]).astype(BF16)
    for w_ref, o_ref in zip(refs[:n], refs[n:]):
        o_ref[...] = jnp.dot(xn, w_ref[...], preferred_element_type=F32).astype(o_ref.dtype)


def _norm_proj(x2, g, ws, out_dtype=F32, tm=256):
    t, d = x2.shape
    tm = min(tm, t)
    in_specs = [pl.BlockSpec((tm, d), lambda i: (i, 0)), pl.BlockSpec((1, d), lambda i: (0, 0))]
    in_specs += [pl.BlockSpec(w.shape, lambda i: (0, 0)) for w in ws]
    outs = pl.pallas_call(
        _inproj_kernel,
        grid=(t // tm,),
        in_specs=in_specs,
        out_specs=[pl.BlockSpec((tm, w.shape[1]), lambda i: (i, 0)) for w in ws],
        out_shape=[jax.ShapeDtypeStruct((t, w.shape[1]), out_dtype) for w in ws],
        compiler_params=_cparams(1),
        name="norm_proj",
    )(x2, g.reshape(1, d), *ws)
    return outs


def _seq_specs(ts, nc, width):
    per = ts // HALO
    nb = nc * per

    def tile(b, d, c):
        return c + d * (nc - 1 - 2 * c)

    main = pl.BlockSpec((None, ts, width), lambda b, d, c: (b, tile(b, d, c), 0))
    prev = pl.BlockSpec((None, HALO, width), lambda b, d, c: (b, jnp.maximum(tile(b, d, c) * per - 1, 0), 0))
    nxt = pl.BlockSpec((None, HALO, width), lambda b, d, c: (b, jnp.minimum((tile(b, d, c) + 1) * per, nb - 1), 0))
    return main, prev, nxt


def _out_spec(ts, nc, width):
    return pl.BlockSpec((None, ts, width), lambda b, d, c: (b, nc - 1 - d * c, 0))


def _dir_spec(shape):
    return pl.BlockSpec((None,) + tuple(shape), lambda b, d, c: (d,) + (0,) * len(shape))


def _const_spec(shape):
    return pl.BlockSpec(tuple(shape), lambda b, d, c: (0,) * len(shape))


LRU_W = 512
LRU_SCAN_BLOCK = 16


def _lru_kernel(u_ref, prev_ref, next_ref, cw_ref, cb_ref, gw_ref, gb_ref, lam_ref, o_ref, hf_ref, carry_ref,
                *, ts, nc):
    d = pl.program_id(1)
    c = pl.program_id(2)

    @pl.when(c == 0)
    def _():
        carry_ref[...] = jnp.zeros_like(carry_ref)

    def run(rev):
        tile = (nc - 1 - c) if rev else c
        xc = _conv_rows(prev_ref[:, :LRU_W], u_ref[:, :LRU_W], next_ref[:, :LRU_W], cw_ref, cb_ref[...], 2,
                        tile == 0, tile == nc - 1)
        gl = _dot(xc, gw_ref[...]) + gb_ref[...]
        r = jax.nn.sigmoid(gl[:, :LRU_W])
        i = jax.nn.sigmoid(gl[:, LRU_W:])
        log_a = -LRU_C * r * _softplus(-lam_ref[...])
        a = jnp.exp(log_a)
        b = jnp.sqrt(1.0 - jnp.exp(2.0 * log_a)) * (i * xc)
        a_cum, h_loc = _scan_rows(a, b, rev, LRU_SCAN_BLOCK)
        nblk = ts // LRU_SCAN_BLOCK
        carry = carry_ref[...]
        hs = {}
        for k in (range(nblk - 1, -1, -1) if rev else range(nblk)):
            rows_k = slice(k * LRU_SCAN_BLOCK, (k + 1) * LRU_SCAN_BLOCK)
            hk = h_loc[rows_k] + a_cum[rows_k] * carry
            carry = hk[0:1, :] if rev else hk[LRU_SCAN_BLOCK - 1:LRU_SCAN_BLOCK, :]
            hs[k] = hk
        carry_ref[...] = carry
        h = jnp.concatenate([hs[k] for k in range(nblk)], axis=0)
        rows = pl.ds(pl.multiple_of(tile * ts, ts), ts)
        if rev:
            o_ref[...] = (hf_ref[rows, :] + h) * _gelu_tanh(u_ref[:, LRU_W:])
        else:
            hf_ref[rows, :] = h

    pl.when(d == 0)(functools.partial(run, False))
    pl.when(d == 1)(functools.partial(run, True))


def _lru(u_a, conv_w, conv_b, gate_w, gate_b, lam, ts):
    bn, s, width = u_a.shape
    nc = s // ts
    main, prev, nxt = _seq_specs(ts, nc, width)
    return pl.pallas_call(
        functools.partial(_lru_kernel, ts=ts, nc=nc),
        grid=(bn, 2, nc),
        in_specs=[main, prev, nxt, _const_spec(conv_w.shape), _const_spec(conv_b.shape),
                  _dir_spec(gate_w.shape[1:]), _dir_spec(gate_b.shape[1:]), _dir_spec(lam.shape[1:])],
        out_specs=_out_spec(ts, nc, LRU_W),
        out_shape=jax.ShapeDtypeStruct((bn, s, LRU_W), F32),
        scratch_shapes=[pltpu.VMEM((s, LRU_W), F32), pltpu.VMEM((1, LRU_W), F32)],
        compiler_params=_cparams(3),
        name="rglru",
    )(u_a, u_a, u_a, conv_w, conv_b, gate_w, gate_b, lam)


GLA_QK = GLA_HEADS * GLA_DK
GLA_V = GLA_HEADS * GLA_DV
GLA_COLS = 2 * GLA_QK + 2 * GLA_V + 2 * LANES


def _gla_kernel(u_ref, wa_ref, ba_ref, ng_ref, o_ref, of_ref, st_ref, *, ts, nc):
    d = pl.program_id(1)
    c = pl.program_id(2)
    nsub = ts // CHUNK

    @pl.when(c == 0)
    def _():
        st_ref[...] = jnp.zeros_like(st_ref)

    def run(rev):
        tile = (nc - 1 - c) if rev else c
        tri = _tri(CHUNK, rev)
        ref_row = CHUNK // 2 if rev else CHUNK // 2 - 1
        last_row = 0 if rev else CHUNK - 1
        code0 = 2 * GLA_QK + 2 * GLA_V + (LANES if rev else 0)
        stacked = (GLA_HEADS * CHUNK, GLA_QK)
        head_mask = (lax.shift_right_logical(lax.broadcasted_iota(jnp.int32, stacked, 0), CHUNK.bit_length() - 1)
                     == lax.shift_right_logical(lax.broadcasted_iota(jnp.int32, stacked, 1), GLA_DK.bit_length() - 1))
        tri_heads = jnp.concatenate([tri] * GLA_HEADS, axis=0)

        stack = lambda x: jnp.where(head_mask, jnp.concatenate([x] * GLA_HEADS, axis=0), 0.0)
        order = list(range(nsub - 1, -1, -1)) if rev else list(range(nsub))

        logits = _dot(u_ref[:, code0:code0 + LANES], wa_ref[...]) + ba_ref[...]
        log_a = -_softplus(-logits) * (1.0 / GLA_TAU)
        work = {}
        for ci in order:
            rows = slice(ci * CHUNK, (ci + 1) * CHUNK)
            q = u_ref[rows, 0:GLA_QK] * (GLA_DK ** -0.5)
            k = u_ref[rows, GLA_QK:2 * GLA_QK]
            v = u_ref[rows, 2 * GLA_QK:2 * GLA_QK + GLA_V]
            b = _dot_exact_rhs(tri, log_a[rows])
            b_mid = b[ref_row:ref_row + 1, :]
            b_end = b[last_row:last_row + 1, :]
            sc = jnp.where(tri_heads, _dot_nt(stack(q * jnp.exp(b - b_mid)), k * jnp.exp(b_mid - b)), 0.0)
            o_intra = _dot(sc, v)
            v_heads = jnp.concatenate([v[:, h * GLA_DV:(h + 1) * GLA_DV] for h in range(GLA_HEADS)], axis=0)
            loc = _dot_tn(v_heads, stack(k * jnp.exp(b_end - b)))
            work[ci] = (o_intra, stack(q * jnp.exp(b)).astype(BF16), jnp.exp(b_end), loc)

        st = st_ref[...]
        outs = {}
        for ci in order:
            o_intra, qe, decay, loc = work[ci]
            o_inter = _dot_nt(qe, st)
            outs[ci] = jnp.concatenate(
                [o_intra[h * CHUNK:(h + 1) * CHUNK, h * GLA_DV:(h + 1) * GLA_DV] + o_inter[h * CHUNK:(h + 1) * CHUNK]
                 for h in range(GLA_HEADS)], axis=1)
            st = st * decay + loc
        st_ref[...] = st

        seq_rows = pl.ds(pl.multiple_of(tile * ts, ts), ts)
        o = jnp.concatenate([outs[ci] for ci in range(nsub)], axis=0)
        if rev:
            o = o + of_ref[seq_rows, :]
            g = u_ref[:, 2 * GLA_QK + GLA_V:2 * GLA_QK + 2 * GLA_V]
            parts = []
            for h in range(GLA_HEADS):
                vs = slice(h * GLA_DV, (h + 1) * GLA_DV)
                parts.append(_rmsnorm(o[:, vs], ng_ref[...]) * _silu(g[:, vs]))
            o_ref[...] = jnp.concatenate(parts, axis=1)
        else:
            of_ref[seq_rows, :] = o

    pl.when(d == 0)(functools.partial(run, False))
    pl.when(d == 1)(functools.partial(run, True))


def _gla(u_b, wa, ba, norm_g, ts):
    bn, s, width = u_b.shape
    nc = s // ts
    main, _, _ = _seq_specs(ts, nc, width)
    return pl.pallas_call(
        functools.partial(_gla_kernel, ts=ts, nc=nc),
        grid=(bn, 2, nc),
        in_specs=[main, _dir_spec(wa.shape[1:]), _dir_spec(ba.shape[1:]), _const_spec(norm_g.shape)],
        out_specs=_out_spec(ts, nc, GLA_V),
        out_shape=jax.ShapeDtypeStruct((bn, s, GLA_V), F32),
        scratch_shapes=[pltpu.VMEM((s, GLA_V), F32), pltpu.VMEM((GLA_DV, GLA_QK), F32)],
        compiler_params=_cparams(3),
        name="gla",
    )(u_b, wa, ba, norm_g)


SSD_W = 512
SSD_BC = SSD_GROUPS * SSD_N
SSD_CONV = SSD_W + 2 * SSD_BC
SSD_COLS = SSD_W + SSD_CONV + LANES
SSD_GW = SSD_W // SSD_GROUPS


def _ssd_kernel(u_ref, prev_ref, next_ref, cw_ref, cb_ref, dtb_ref, alog_ref, dsk_ref, ng_ref, o_ref,
                yf_ref, st_ref, *, ts, nc):
    d = pl.program_id(1)
    c = pl.program_id(2)
    nsub = ts // CHUNK
    xbc = slice(SSD_W, SSD_W + SSD_CONV)
    dtc = slice(SSD_W + SSD_CONV, SSD_COLS)

    @pl.when(c == 0)
    def _():
        st_ref[...] = jnp.zeros_like(st_ref)

    def run(rev):
        tile = (nc - 1 - c) if rev else c
        tri = _tri(CHUNK, rev)
        last_row = 0 if rev else CHUNK - 1
        xs = _silu(_conv_rows(prev_ref[:, xbc], u_ref[:, xbc], next_ref[:, xbc], cw_ref, cb_ref[...], 2,
                              tile == 0, tile == nc - 1))
        lane = lax.broadcasted_iota(jnp.int32, (1, LANES), 1)
        a_row = jnp.where(lane < 2 * SSD_HEADS, -jnp.exp(alog_ref[...]), 0.0)
        er = lax.broadcasted_iota(jnp.int32, (LANES, SSD_W), 0)
        ec = lax.broadcasted_iota(jnp.int32, (LANES, SSD_W), 1)
        expand = er == (lax.shift_right_logical(ec, SSD_P.bit_length() - 1) + (SSD_HEADS if rev else 0))
        h0 = SSD_HEADS if rev else 0
        order = list(range(nsub - 1, -1, -1)) if rev else list(range(nsub))
        rl = lax.broadcasted_iota(jnp.int32, (ts, ts), 0)
        rs = lax.broadcasted_iota(jnp.int32, (ts, ts), 1)
        shift = CHUNK.bit_length() - 1
        same_chunk = lax.shift_right_logical(rl, shift) == lax.shift_right_logical(rs, shift)
        tri_tile = same_chunk & ((rs >= rl) if rev else (rs <= rl))

        dt = _softplus(u_ref[:, dtc] + dtb_ref[...])
        cs = _dot_exact_rhs(tri_tile, dt * a_row)
        cs_t = _transpose_exact(cs)
        both = _dot_exact_lhs(jnp.concatenate([dt, cs], axis=0), expand)
        dt_e, cs_e = both[:ts], both[ts:]
        work = {}
        for ci in order:
            rows = slice(ci * CHUNK, (ci + 1) * CHUNK)
            x = xs[rows, 0:SSD_W]
            bm = xs[rows, SSD_W:SSD_W + SSD_BC]
            cm = xs[rows, SSD_W + SSD_BC:SSD_CONV]
            cs_c = cs_e[rows]
            cs_end = cs_c[last_row:last_row + 1, :]
            xdt = x * dt_e[rows]
            xdec = xdt * jnp.exp(cs_end - cs_c)
            intra, locs = [], []
            for g in range(SSD_GROUPS):
                ns = slice(g * SSD_N, (g + 1) * SSD_N)
                gs = slice(g * SSD_GW, (g + 1) * SSD_GW)
                cb = _dot_nt(cm[:, ns], bm[:, ns])
                for r in range(SSD_HEADS // SSD_GROUPS):
                    hh = g * (SSD_HEADS // SSD_GROUPS) + r
                    j = h0 + hh
                    seg = cs[rows, j:j + 1] - cs_t[j:j + 1, rows]
                    lmat = jnp.exp(jnp.where(tri, seg, -jnp.inf))
                    intra.append(_dot(cb * lmat, xdt[:, hh * SSD_P:(hh + 1) * SSD_P]))
                locs.append(_dot_tn(bm[:, ns], xdec[:, gs]))
            work[ci] = (jnp.concatenate(intra, axis=1), cm.astype(BF16), jnp.exp(cs_c), jnp.exp(cs_end),
                        jnp.concatenate(locs, axis=1))

        st = st_ref[...]
        ys = {}
        for ci in order:
            y_intra, cm, grow, decay, loc = work[ci]
            y_inter = jnp.concatenate(
                [_dot(cm[:, g * SSD_N:(g + 1) * SSD_N], st[:, g * SSD_GW:(g + 1) * SSD_GW])
                 for g in range(SSD_GROUPS)], axis=1)
            ys[ci] = y_intra + y_inter * grow
            st = st * decay + loc
        st_ref[...] = st

        y = jnp.concatenate([ys[ci] for ci in range(nsub)], axis=0)
        seq_rows = pl.ds(pl.multiple_of(tile * ts, ts), ts)
        if rev:
            y = y + yf_ref[seq_rows, :] + xs[:, 0:SSD_W] * dsk_ref[...]
            y = y * _silu(u_ref[:, 0:SSD_W])
            parts = []
            for g in range(SSD_GROUPS):
                gs = slice(g * SSD_GW, (g + 1) * SSD_GW)
                parts.append(_rmsnorm(y[:, gs], ng_ref[:, gs]))
            o_ref[...] = jnp.concatenate(parts, axis=1)
        else:
            yf_ref[seq_rows, :] = y

    pl.when(d == 0)(functools.partial(run, False))
    pl.when(d == 1)(functools.partial(run, True))


def _ssd(u_c, conv_w, conv_b, dt_bias, a_log, d_skip, norm_g, ts):
    bn, s, width = u_c.shape
    nc = s // ts
    main, prev, nxt = _seq_specs(ts, nc, width)
    consts = [conv_w, conv_b, dt_bias, a_log, d_skip, norm_g]
    return pl.pallas_call(
        functools.partial(_ssd_kernel, ts=ts, nc=nc),
        grid=(bn, 2, nc),
        in_specs=[main, prev, nxt] + [_const_spec(a.shape) for a in consts],
        out_specs=_out_spec(ts, nc, SSD_W),
        out_shape=jax.ShapeDtypeStruct((bn, s, SSD_W), F32),
        scratch_shapes=[pltpu.VMEM((s, SSD_W), F32), pltpu.VMEM((SSD_N, SSD_W), F32)],
        compiler_params=_cparams(3),
        name="ssd",
    )(u_c, u_c, u_c, *consts)


HY_W = 512


def _hyena_pre_kernel(u_ref, prev_ref, next_ref, cw_ref, cb_ref, x0_ref, p_ref, *, nc):
    c = pl.program_id(1)
    uc = _conv_rows(prev_ref[...], u_ref[...], next_ref[...], cw_ref, cb_ref[...], 1, c == 0, c == nc - 1)
    x0_ref[...] = uc[:, 0:HY_W]
    p_ref[...] = uc[:, 2 * HY_W:3 * HY_W] * uc[:, HY_W:2 * HY_W]


def _hyena_pre(u_d, conv_w, conv_b, ts):
    bn, s, width = u_d.shape
    nc = s // ts
    per = ts // HALO
    nb = nc * per
    main = pl.BlockSpec((None, ts, width), lambda b, c: (b, c, 0))
    prev = pl.BlockSpec((None, HALO, width), lambda b, c: (b, jnp.maximum(c * per - 1, 0), 0))
    nxt = pl.BlockSpec((None, HALO, width), lambda b, c: (b, jnp.minimum((c + 1) * per, nb - 1), 0))
    out = pl.BlockSpec((None, ts, HY_W), lambda b, c: (b, c, 0))
    return pl.pallas_call(
        functools.partial(_hyena_pre_kernel, nc=nc),
        grid=(bn, nc),
        in_specs=[main, prev, nxt, pl.BlockSpec(conv_w.shape, lambda b, c: (0, 0)),
                  pl.BlockSpec(conv_b.shape, lambda b, c: (0, 0))],
        out_specs=[out, out],
        out_shape=[jax.ShapeDtypeStruct((bn, s, HY_W), F32)] * 2,
        compiler_params=_cparams(2),
        name="hyena_pre",
    )(u_d, u_d, u_d, conv_w, conv_b)


def _hyena_positions(length):
    t = np.linspace(0.0, 1.0, length)[:, None]
    w = (2.0 * math.pi / length) * np.arange(length)[:, None]
    f = np.linspace(1e-4, HY_BANDS - 1, HY_BANDS)[None, :]
    z = np.concatenate([t, np.cos(f * w), -np.sin(f * w)], axis=-1)
    z = np.concatenate([z, z[:1], z[:0:-1]], axis=0)
    return jnp.asarray(np.pad(z, ((0, 0), (0, LANES - HY_EMB))), F32)


def _hyena_filter_kernel(z_ref, w1_ref, b1_ref, w2_ref, b2_ref, w3_ref, b3_ref, w4_ref, fr_ref, dc_ref, o_ref):
    i = pl.program_id(0)
    half = pl.num_programs(0) // 2
    z = z_ref[...]
    h = jnp.sin(fr_ref[0:1, :] * (_dot(z, w1_ref[...]) + b1_ref[...]))
    h = jnp.sin(fr_ref[1:2, :] * (_dot(h, w2_ref[...]) + b2_ref[...]))
    h = jnp.sin(fr_ref[2:3, :] * (_dot(h, w3_ref[...]) + b3_ref[...]))
    banks = _dot(h, w4_ref[...]) * jnp.exp(-z[:, 0:1] * jnp.abs(dc_ref[...]))
    out = jnp.where(i < half, banks[:, :HY_W], banks[:, HY_W:])
    row = lax.broadcasted_iota(jnp.int32, out.shape, 0)
    o_ref[...] = jnp.where((i == half) & (row == 0), 0.0, out)


def _hyena_filters(z, w1, b1, w2, b2, w3, b3, w4, freq, decay, tl=512):
    n = z.shape[0]
    tl = min(tl, n // 2)
    consts = [jnp.pad(w1, ((0, LANES - HY_EMB), (0, 0))), b1.reshape(1, -1), w2, b2.reshape(1, -1), w3,
              b3.reshape(1, -1), w4, freq, decay.reshape(1, -1)]
    return pl.pallas_call(
        _hyena_filter_kernel,
        grid=(n // tl,),
        in_specs=[pl.BlockSpec((tl, LANES), lambda i: (i, 0))]
        + [pl.BlockSpec(a.shape, lambda i: (0, 0)) for a in consts],
        out_specs=pl.BlockSpec((tl, HY_W), lambda i: (i, 0)),
        out_shape=jax.ShapeDtypeStruct((n, HY_W), F32),
        compiler_params=_cparams(1),
        name="hyena_filter",
    )(z, *consts)


FFT_N2 = 128


def _fft_tables(length):
    n = 2 * length
    n2 = FFT_N2
    n1 = n // n2
    k = np.arange(n1)
    idx = (n2 * np.outer(k, np.arange(n1))[None] + np.arange(n2)[:, None, None] * k[None, :, None]) % n
    ang = -2.0 * math.pi * idx / n
    stage1 = np.concatenate([np.cos(ang), np.sin(ang)], axis=1)
    ang2 = -2.0 * math.pi * (np.outer(np.arange(n2), np.arange(n2)) % n2) / n2
    fr, fi = np.cos(ang2), np.sin(ang2)
    stage2 = np.block([[fr, -fi], [fi, fr]])
    stage2_inv = np.block([[fr, fi], [-fi, fr]])
    angt = np.transpose(ang, (0, 2, 1))[:, :n1 // 2, :]
    stage3 = np.concatenate([np.cos(angt), np.sin(angt)], axis=2) / n
    as_bf16 = lambda a: jnp.asarray(a, F32).astype(BF16)
    return dict(stage1=as_bf16(stage1), stage2=as_bf16(stage2), stage2_inv=as_bf16(stage2_inv),
                stage3=as_bf16(stage3))


def _fft_stage1_kernel(x_ref, f_ref, o_ref):
    n1 = o_ref.shape[1]
    for j in range(x_ref.shape[1]):
        a = jnp.dot(f_ref[j], x_ref[:, j, :].astype(BF16), preferred_element_type=F32)
        o_ref[0, :, j, :] = a[:n1]
        o_ref[1, :, j, :] = a[n1:]


def _fft_stage1(x4, table, tn2=8):
    bn, kk, n2, ch = x4.shape
    n1 = table.shape[1] // 2
    return pl.pallas_call(
        _fft_stage1_kernel,
        grid=(n2 // tn2, bn),
        in_specs=[pl.BlockSpec((None, kk, tn2, ch), lambda i, b: (b, 0, i, 0)),
                  pl.BlockSpec((tn2, 2 * n1, kk), lambda i, b: (i, 0, 0))],
        out_specs=pl.BlockSpec((None, 2, n1, tn2, ch), lambda i, b: (b, 0, 0, i, 0)),
        out_shape=jax.ShapeDtypeStruct((bn, 2, n1, n2, ch), F32),
        compiler_params=_cparams(2),
        name="fft_stage1",
    )(x4, table)


def _fft_stage2_kernel(a_ref, e_ref, *rest, with_filter):
    if with_filter:
        h_ref, ei_ref, o_ref = rest
    else:
        (o_ref,) = rest
    n2 = a_ref.shape[2]
    for j in range(a_ref.shape[1]):
        x = jnp.dot(e_ref[...], jnp.concatenate([a_ref[0, j], a_ref[1, j]], axis=0).astype(BF16),
                    preferred_element_type=F32)
        if with_filter:
            xr, xi = x[:n2], x[n2:]
            hr, hi = h_ref[0, j], h_ref[1, j]
            z = jnp.concatenate([xr * hr - xi * hi, xr * hi + xi * hr], axis=0)
            x = jnp.dot(ei_ref[...], z.astype(BF16), preferred_element_type=F32)
        o_ref[0, j] = x[:n2]
        o_ref[1, j] = x[n2:]


def _fft_stage2(a, tables, spectrum=None, tk1=4):
    bn, _, n1, n2, ch = a.shape
    tk1 = min(tk1, n1)
    blk = lambda bmap: pl.BlockSpec((None, 2, tk1, n2, ch), bmap)
    mat = pl.BlockSpec((2 * n2, 2 * n2), lambda i, b: (0, 0))
    data_map = lambda i, b: (b, 0, i, 0, 0)
    if spectrum is None:
        ins, specs = [a, tables["stage2"]], [blk(data_map), mat]
    else:
        ins = [a, tables["stage2"], spectrum, tables["stage2_inv"]]
        specs = [blk(data_map), mat, blk(lambda i, b: (0, 0, i, 0, 0)), mat]
    return pl.pallas_call(
        functools.partial(_fft_stage2_kernel, with_filter=spectrum is not None),
        grid=(n1 // tk1, bn),
        in_specs=specs,
        out_specs=blk(data_map),
        out_shape=jax.ShapeDtypeStruct(a.shape, F32),
        compiler_params=_cparams(2),
        name="fft_stage2",
    )(*ins)


def _fft_stage3_kernel(b_ref, g_ref, p_ref, x0_ref, bias_ref, o_ref):
    for j in range(p_ref.shape[1]):
        rhs = jnp.concatenate([b_ref[0, :, j, :], b_ref[1, :, j, :]], axis=0).astype(BF16)
        y = jnp.dot(g_ref[j], rhs, preferred_element_type=F32)
        o_ref[:, j, :] = x0_ref[:, j, :] * (y + p_ref[:, j, :] * bias_ref[...])


def _fft_stage3(bm, table, p4, x04, bias, tn2=8):
    bn, kk, n2, ch = p4.shape
    n1 = bm.shape[2]
    seq = pl.BlockSpec((None, kk, tn2, ch), lambda i, b: (b, 0, i, 0))
    return pl.pallas_call(
        _fft_stage3_kernel,
        grid=(n2 // tn2, bn),
        in_specs=[pl.BlockSpec((None, 2, n1, tn2, ch), lambda i, b: (b, 0, 0, i, 0)),
                  pl.BlockSpec((tn2, kk, 2 * n1), lambda i, b: (i, 0, 0)),
                  seq, seq, pl.BlockSpec((1, ch), lambda i, b: (0, 0))],
        out_specs=seq,
        out_shape=jax.ShapeDtypeStruct(p4.shape, F32),
        compiler_params=_cparams(2),
        name="fft_stage3",
    )(bm, table, p4, x04, bias.reshape(1, ch))


def _long_conv_gated(p, x0, filt, bias, tables):
    bn, length, ch = p.shape
    n2 = FFT_N2
    n1 = 2 * length // n2
    spectrum = _fft_stage2(_fft_stage1(filt.reshape(1, n1, n2, ch), tables["stage1"]), tables)
    p4 = p.reshape(bn, n1 // 2, n2, ch)
    a = _fft_stage1(p4, tables["stage1"][:, :, :n1 // 2])
    bm = _fft_stage2(a, tables, spectrum)
    y = _fft_stage3(bm, tables["stage3"], p4, x0.reshape(bn, n1 // 2, n2, ch), bias)
    return y.reshape(bn, length, ch)


def _merge_kernel(x_ref, g_ref, ya_ref, yb_ref, yc_ref, yd_ref, wg_ref, wb_ref, wo_ref, o_ref):
    x = x_ref[...]
    d = x.shape[1]
    xn = _rmsnorm(x, g_ref[...]).astype(BF16)
    merged = None
    for n, y_ref in enumerate((ya_ref, yb_ref, yc_ref, yd_ref)):
        gate = jax.nn.sigmoid(jnp.dot(xn, wg_ref[:, n * d:(n + 1) * d], preferred_element_type=F32))
        term = gate * _dot(y_ref[...], wb_ref[n])
        merged = term if merged is None else merged + term
    o_ref[...] = x + _dot(merged, wo_ref[...])


def _merge(x2, g, ys, w_gate, w_branch, w_out, tm=512):
    t, d = x2.shape
    tm = min(tm, t)
    row = lambda w: pl.BlockSpec((tm, w), lambda i: (i, 0))
    full = lambda a: pl.BlockSpec(a.shape, lambda i: (0,) * a.ndim)
    return pl.pallas_call(
        _merge_kernel,
        grid=(t // tm,),
        in_specs=[row(d), pl.BlockSpec((1, d), lambda i: (0, 0))] + [row(y.shape[1]) for y in ys]
        + [full(w_gate), full(w_branch), full(w_out)],
        out_specs=row(d),
        out_shape=jax.ShapeDtypeStruct((t, d), F32),
        compiler_params=_cparams(1),
        name="merge",
    )(x2, g.reshape(1, d), *ys, w_gate, w_branch, w_out)


def _xattn_kernel(x_ref, g_ref, kv_ref, wq_ref, wo_ref, o_ref):
    x = x_ref[...]
    d = x.shape[1]
    hd = d // XA_HEADS
    q = _dot(_rmsnorm(x, g_ref[...]), wq_ref[...])
    outs = []
    for h in range(XA_HEADS):
        k = kv_ref[:, h * hd:(h + 1) * hd]
        v = kv_ref[:, d + h * hd:d + (h + 1) * hd]
        s = _dot_nt(q[:, h * hd:(h + 1) * hd], k) * (hd ** -0.5)
        s = s - jnp.max(s, axis=-1, keepdims=True)
        e = jnp.exp(s)
        p = e / jnp.sum(e, axis=-1, keepdims=True)
        outs.append(_dot(p, v))
    o_ref[...] = x + _dot(jnp.concatenate(outs, axis=1), wo_ref[...])


def _xattn(x3, g, kv, wq, wo, tq=512):
    bn, s, d = x3.shape
    m = kv.shape[1]
    tq = min(tq, s)
    return pl.pallas_call(
        _xattn_kernel,
        grid=(bn, s // tq),
        in_specs=[pl.BlockSpec((None, tq, d), lambda b, i: (b, i, 0)),
                  pl.BlockSpec((1, d), lambda b, i: (0, 0)),
                  pl.BlockSpec((None, m, 2 * d), lambda b, i: (b, 0, 0)),
                  pl.BlockSpec(wq.shape, lambda b, i: (0, 0)),
                  pl.BlockSpec(wo.shape, lambda b, i: (0, 0))],
        out_specs=pl.BlockSpec((None, tq, d), lambda b, i: (b, i, 0)),
        out_shape=jax.ShapeDtypeStruct((bn, s, d), F32),
        compiler_params=_cparams(2),
        name="xattn",
    )(x3, g.reshape(1, d), kv, wq, wo)


def _router_kernel(x_ref, g_ref, wr_ref, aff_ref):
    xn = _rmsnorm(x_ref[...], g_ref[...])
    logits = jnp.dot(xn.astype(BF16), wr_ref[...], preferred_element_type=F32)
    lane = lax.broadcasted_iota(jnp.int32, logits.shape, 1)
    logits = jnp.where(lane < N_EXPERTS, logits, -jnp.inf)
    e = jnp.exp(logits - jnp.max(logits, axis=-1, keepdims=True))
    aff_ref[...] = e / jnp.sum(e, axis=-1, keepdims=True)


def _router(x2, g, wr_pad, tm=512):
    t, d = x2.shape
    tm = min(tm, t)
    return pl.pallas_call(
        _router_kernel,
        grid=(t // tm,),
        in_specs=[pl.BlockSpec((tm, d), lambda i: (i, 0)), pl.BlockSpec((1, d), lambda i: (0, 0)),
                  pl.BlockSpec(wr_pad.shape, lambda i: (0, 0))],
        out_specs=pl.BlockSpec((tm, LANES), lambda i: (i, 0)),
        out_shape=jax.ShapeDtypeStruct((t, LANES), F32),
        compiler_params=_cparams(1),
        name="router",
    )(x2, g.reshape(1, d), wr_pad)


def _expert_kernel(idx_ref, gate_ref, g_ref, wg_ref, wu_ref, wd_ref, x_hbm, acc_in_hbm, out_hbm,
                   xbuf, abuf, sems):
    del acc_in_hbm
    tm = xbuf.shape[0]

    def x_row(r):
        return pltpu.make_async_copy(x_hbm.at[pl.ds(idx_ref[0, r], 1)], xbuf.at[pl.ds(r, 1)], sems.at[0])

    def acc_row_in(r):
        return pltpu.make_async_copy(out_hbm.at[pl.ds(idx_ref[0, r], 1)], abuf.at[pl.ds(r, 1)], sems.at[1])

    def acc_row_out(r):
        return pltpu.make_async_copy(abuf.at[pl.ds(r, 1)], out_hbm.at[pl.ds(idx_ref[0, r], 1)], sems.at[2])

    def start_gather(r, carry):
        x_row(r).start()
        acc_row_in(r).start()
        return carry

    def start_scatter(r, carry):
        acc_row_out(r).start()
        return carry

    lax.fori_loop(0, tm, start_gather, 0, unroll=8)
    pltpu.make_async_copy(x_hbm.at[pl.ds(0, tm)], xbuf, sems.at[0]).wait()
    xe = _rmsnorm(xbuf[...], g_ref[...]).astype(BF16)
    hdn = _silu(jnp.dot(xe, wg_ref[...], preferred_element_type=F32)) * jnp.dot(
        xe, wu_ref[...], preferred_element_type=F32)
    ye = _dot(hdn, wd_ref[...]) * gate_ref[...]
    pltpu.make_async_copy(out_hbm.at[pl.ds(0, tm)], abuf, sems.at[1]).wait()
    abuf[...] = abuf[...] + ye
    lax.fori_loop(0, tm, start_scatter, 0, unroll=8)
    pltpu.make_async_copy(abuf, out_hbm.at[pl.ds(0, tm)], sems.at[2]).wait()


def _experts(x2, g, idx, gate, w_gate, w_up, w_down, tm=1024):
    t, d = x2.shape
    ne, cap = idx.shape
    tm = min(tm, cap)
    nt = cap // tm
    wspec = lambda a: pl.BlockSpec((None,) + a.shape[1:], lambda e, i: (e, 0, 0))
    return pl.pallas_call(
        _expert_kernel,
        grid=(ne, nt),
        in_specs=[pl.BlockSpec((None, 1, tm), lambda e, i: (e * nt + i, 0, 0), memory_space=pltpu.SMEM),
                  pl.BlockSpec((None, tm, 1), lambda e, i: (e, i, 0)),
                  pl.BlockSpec((1, d), lambda e, i: (0, 0)),
                  wspec(w_gate), wspec(w_up), wspec(w_down),
                  pl.BlockSpec(memory_space=pl.ANY), pl.BlockSpec(memory_space=pl.ANY)],
        out_specs=pl.BlockSpec(memory_space=pl.ANY),
        out_shape=jax.ShapeDtypeStruct((t, d), F32),
        scratch_shapes=[pltpu.VMEM((tm, d), F32), pltpu.VMEM((tm, d), F32), pltpu.SemaphoreType.DMA((3,))],
        input_output_aliases={7: 0},
        compiler_params=_cparams(2),
        name="experts",
    )(idx.reshape(ne * nt, 1, tm), gate[..., None], g.reshape(1, d), w_gate, w_up, w_down, x2, x2)


def _final_norm_kernel(x_ref, g_ref, o_ref):
    o_ref[...] = _rmsnorm(x_ref[...], g_ref[...])


def _final_norm(x2, g, tm=512):
    t, d = x2.shape
    tm = min(tm, t)
    return pl.pallas_call(
        _final_norm_kernel,
        grid=(t // tm,),
        in_specs=[pl.BlockSpec((tm, d), lambda i: (i, 0)), pl.BlockSpec((1, d), lambda i: (0, 0))],
        out_specs=pl.BlockSpec((tm, d), lambda i: (i, 0)),
        out_shape=jax.ShapeDtypeStruct((t, d), F32),
        compiler_params=_cparams(1),
        name="final_norm",
    )(x2, g.reshape(1, d))


def _pad_cols(w, n):
    return jnp.pad(w, ((0, 0), (0, n - w.shape[1])))


def _block_diag(w):
    h, blk, _ = w.shape
    eye = jnp.eye(h, dtype=w.dtype)
    return jnp.einsum("hij,hk->hikj", w, eye).reshape(h * blk, h * blk)


def _row128(v):
    return _pad_cols(v.reshape(1, -1), LANES)


def _prep_layer(p, l):
    d = p["w_in"].shape[1]
    w_in = p["w_in"][l]
    sizes = (512, 512, 256, 256, 512, 512, 32, 512, 768, 16, 1536, 4 * d)
    offs = np.concatenate([[0], np.cumsum(sizes)])
    seg = [w_in[:, offs[i]:offs[i + 1]] for i in range(len(sizes))]
    a_x, a_y, b_q, b_k, b_v, b_g, b_lr, c_z, c_xbc, c_dt, d_u, gates = seg
    w_a = jnp.concatenate([a_x, a_y], axis=1)
    w_b = jnp.concatenate([b_q, b_k, b_v, b_g, _pad_cols(b_lr[:, :GLA_RANK], LANES),
                           _pad_cols(b_lr[:, GLA_RANK:], LANES)], axis=1)
    w_c = jnp.concatenate([c_z, c_xbc, _pad_cols(c_dt, LANES)], axis=1)
    out = {
        "w_mix": [w.astype(BF16) for w in (w_a, w_b, w_c, d_u)],
        "w_gates": gates.astype(BF16),
        "lru_gate_w": jnp.stack([jnp.concatenate([_block_diag(p["lru_gate_w"][l, dd, 0]),
                                                  _block_diag(p["lru_gate_w"][l, dd, 1])], axis=1)
                                 for dd in range(2)]).astype(BF16),
        "lru_gate_b": p["lru_gate_b"][l].reshape(2, 1, 2 * LRU_W),
        "lru_lambda": p["lru_lambda"][l].reshape(2, 1, LRU_W),
        "lru_conv_w": p["lru_conv_w"][l],
        "lru_conv_b": p["lru_conv_b"][l].reshape(1, -1),
        "gla_wa": jnp.pad(p["gla_wa2"][l], ((0, 0), (0, LANES - GLA_RANK), (0, 0))).astype(BF16),
        "gla_ba": p["gla_ba"][l].reshape(2, 1, GLA_QK),
        "gla_norm": p["gla_norm"][l].reshape(1, GLA_DV),
        "ssd_conv_w": p["ssd_conv_w"][l],
        "ssd_conv_b": p["ssd_conv_b"][l].reshape(1, -1),
        "ssd_dt_bias": _row128(p["ssd_dt_bias"][l]),
        "ssd_a_log": _row128(p["ssd_a_log"][l]),
        "ssd_d": jnp.repeat(p["ssd_d"][l], SSD_P).reshape(1, SSD_W),
        "ssd_norm": p["ssd_norm"][l].reshape(1, SSD_W),
        "hy_conv_w": p["hy_conv_w"][l],
        "hy_conv_b": p["hy_conv_b"][l].reshape(1, -1),
        "w_branch": p["w_branch"][l].astype(BF16),
        "w_out": p["w_out"][l].astype(BF16),
        "xa_wq": p["xa_wq"][l].astype(BF16),
        "xa_wkv": p["xa_wkv"][l].astype(BF16),
        "xa_wo": p["xa_wo"][l].astype(BF16),
        "router_w": _pad_cols(p["router_w"][l], LANES).astype(BF16),
        "exp_w_gate": p["exp_w_gate"][l].astype(BF16),
        "exp_w_up": p["exp_w_up"][l].astype(BF16),
        "exp_w_down": p["exp_w_down"][l].astype(BF16),
    }
    out.update({k: p[k][l] for k in RAW_KEYS})
    return out


RAW_KEYS = ("norm_mix", "norm_xa", "norm_mem", "norm_ffn", "hy_w1", "hy_b1", "hy_w2", "hy_b2", "hy_w3", "hy_b3",
            "hy_w4", "hy_freq", "hy_decay", "hy_bias")


def _seq_tile(s):
    return min(256, s)


def _seq_consts(s):
    return dict(z=_hyena_positions(s), fft=_fft_tables(s))


def _hyena(u_d, q, consts, ts):
    x0, pg = _hyena_pre(u_d, q["hy_conv_w"], q["hy_conv_b"], ts)
    filt = _hyena_filters(consts["z"], q["hy_w1"], q["hy_b1"], q["hy_w2"], q["hy_b2"], q["hy_w3"], q["hy_b3"],
                          q["hy_w4"], q["hy_freq"], q["hy_decay"])
    return _long_conv_gated(pg, x0, filt, q["hy_bias"], consts["fft"])


def _xattn_layer(x3, mem, q):
    bn, m, d = mem.shape
    (kv,) = _norm_proj(mem.reshape(bn * m, d), q["norm_mem"], [q["xa_wkv"]], out_dtype=BF16)
    return _xattn(x3, q["norm_xa"], kv.reshape(bn, m, 2 * d), q["xa_wq"], q["xa_wo"])


def _moe_layer(x2, q):
    t, d = x2.shape
    cap = max(1, EC_CAPACITY * t // N_EXPERTS)
    aff = _router(x2, q["norm_ffn"], q["router_w"])
    gate, idx = lax.top_k(aff[:, :N_EXPERTS].T, cap)
    return _experts(x2, q["norm_ffn"], idx, gate, q["exp_w_gate"], q["exp_w_up"], q["exp_w_down"])


def _layer(x, mem, q, consts):
    bn, s, d = x.shape
    t = bn * s
    ts = _seq_tile(s)
    x2 = x.reshape(t, d)

    u_a, u_b, u_c, u_d = _norm_proj(x2, q["norm_mix"], q["w_mix"])
    y_a = _lru(u_a.reshape(bn, s, -1), q["lru_conv_w"], q["lru_conv_b"], q["lru_gate_w"], q["lru_gate_b"],
               q["lru_lambda"], ts)
    y_b = _gla(u_b.reshape(bn, s, -1), q["gla_wa"], q["gla_ba"], q["gla_norm"], ts)
    y_c = _ssd(u_c.reshape(bn, s, -1), q["ssd_conv_w"], q["ssd_conv_b"], q["ssd_dt_bias"], q["ssd_a_log"],
               q["ssd_d"], q["ssd_norm"], ts)
    y_d = _hyena(u_d.reshape(bn, s, -1), q, consts, ts)
    ys = [y.reshape(t, -1) for y in (y_a, y_b, y_c, y_d)]
    x2 = _merge(x2, q["norm_mix"], ys, q["w_gates"], q["w_branch"], q["w_out"])

    x3 = _xattn_layer(x2.reshape(bn, s, d), mem, q)

    return _moe_layer(x3.reshape(t, d), q).reshape(bn, s, d)


def kernel(x_prompt, x_sample, mem_prompt, mem_sample, norm_mix, w_in, lru_conv_w, lru_conv_b, lru_gate_w, lru_gate_b, lru_lambda, gla_wa2, gla_ba, gla_norm, ssd_conv_w, ssd_conv_b, ssd_dt_bias, ssd_a_log, ssd_d, ssd_norm, hy_conv_w, hy_conv_b, hy_w1, hy_b1, hy_w2, hy_b2, hy_w3, hy_b3, hy_w4, hy_freq, hy_decay, hy_bias, w_branch, w_out, norm_xa, norm_mem, xa_wq, xa_wkv, xa_wo, norm_ffn, router_w, exp_w_gate, exp_w_up, exp_w_down, final_norm):
    p = dict(norm_mix=norm_mix, w_in=w_in, lru_conv_w=lru_conv_w, lru_conv_b=lru_conv_b, lru_gate_w=lru_gate_w,
             lru_gate_b=lru_gate_b, lru_lambda=lru_lambda, gla_wa2=gla_wa2, gla_ba=gla_ba, gla_norm=gla_norm,
             ssd_conv_w=ssd_conv_w, ssd_conv_b=ssd_conv_b, ssd_dt_bias=ssd_dt_bias, ssd_a_log=ssd_a_log,
             ssd_d=ssd_d, ssd_norm=ssd_norm, hy_conv_w=hy_conv_w, hy_conv_b=hy_conv_b, hy_w1=hy_w1, hy_b1=hy_b1,
             hy_w2=hy_w2, hy_b2=hy_b2, hy_w3=hy_w3, hy_b3=hy_b3, hy_w4=hy_w4, hy_freq=hy_freq, hy_decay=hy_decay,
             hy_bias=hy_bias, w_branch=w_branch, w_out=w_out, norm_xa=norm_xa, norm_mem=norm_mem, xa_wq=xa_wq,
             xa_wkv=xa_wkv, xa_wo=xa_wo, norm_ffn=norm_ffn, router_w=router_w, exp_w_gate=exp_w_gate,
             exp_w_up=exp_w_up, exp_w_down=exp_w_down)
    depth = w_in.shape[0]
    stacked = jax.tree.map(lambda *a: jnp.stack(a), *[_prep_layer(p, l) for l in range(depth)])
    outs = []
    for x, mem in ((x_prompt, mem_prompt), (x_sample, mem_sample)):
        consts = _seq_consts(x.shape[1])
        x, _ = lax.scan(lambda xc, q, mem=mem, consts=consts: (_layer(xc, mem, q, consts), None), x, stacked)
        bn, s, d = x.shape
        outs.append(_final_norm(x.reshape(bn * s, d), final_norm).reshape(bn, s, d))
    return tuple(outs)
```
